```python
import math
import jax, jax.numpy as jnp
from jax import lax
import numpy as np

D_MODEL = 1024
BATCH = 8
SEQ = 2048
DEPTH = 4
DEC_BATCH = 32
DEC_SEQ = 1
PAST_LEN = 8192
PAGE_SIZE = 128

F32 = jnp.float32
N_MIXERS = 3
N_A = (DEPTH + 2) // 3
N_B = (DEPTH + 1) // 3
N_C = DEPTH // 3
HEAD_DIM = 64
ROPE_THETA = 10000.0
EPS = 1e-6

A_HEADS = D_MODEL // HEAD_DIM
A_WIDTH = A_HEADS * HEAD_DIM
MOBA_BLOCK = 256
MOBA_TOPK = 3
MOBA_QCHUNK = 8

B_HEADS = D_MODEL // (2 * HEAD_DIM)
B_VDIM = 2 * HEAD_DIM
B_WIDTH = B_HEADS * B_VDIM
ATTN_QBLOCK = 128

C_PATTERNS = ((128, 1), (512, 4), (2048, 16))
C_GROUPS = len(C_PATTERNS)
C_HEADS = D_MODEL // HEAD_DIM
C_WIDTH = C_HEADS * HEAD_DIM

kernel_name = 'hybrid_moba_diff_dilated_step'


def rms_norm(x, g):
    xf = x.astype(F32)
    y = xf * lax.rsqrt(jnp.mean(xf * xf, axis=-1, keepdims=True) + EPS)
    return (y * g.astype(F32)).astype(x.dtype)


def rope(x, pos):
    half = x.shape[-1] // 2
    inv = ROPE_THETA ** (-jnp.arange(half, dtype=F32) / half)
    ang = pos.astype(F32)[:, None] * inv[None, :]
    shape = (1, pos.shape[0]) + (1,) * (x.ndim - 3) + (half,)
    cos = jnp.cos(ang).reshape(shape)
    sin = jnp.sin(ang).reshape(shape)
    xf = x.astype(F32)
    x1, x2 = xf[..., :half], xf[..., half:]
    return jnp.concatenate([x1 * cos - x2 * sin, x2 * cos + x1 * sin], axis=-1).astype(x.dtype)


def project_in(h, g, w_in):
    return jnp.einsum('btd,de->bte', rms_norm(h, g), w_in)


def gated_out(o, gate, w_out):
    n, t = gate.shape[0], gate.shape[1]
    z = o.reshape(n, t, -1).astype(gate.dtype) * jax.nn.silu(gate)
    return jnp.einsum('bte,ed->btd', z, w_out)


def gather_pages(pool, page_table):
    g = pool[page_table]
    return g.reshape((g.shape[0], g.shape[1] * g.shape[2]) + g.shape[3:])


def moba_qkv(u, pos, qn, kn):
    n, t, _ = u.shape
    q, k, v, gate = jnp.split(u, 4, axis=-1)
    q = rope(rms_norm(q.reshape(n, t, A_HEADS, HEAD_DIM), qn), pos)
    k = rope(rms_norm(k.reshape(n, t, A_HEADS, HEAD_DIM), kn), pos)
    v = v.reshape(n, t, A_HEADS, HEAD_DIM)
    return q, k, v, gate


def moba_attend(q, k, v, q_pos):
    n, nq, h, dh = q.shape
    L = k.shape[1]
    nb = L // MOBA_BLOCK
    ksel = min(MOBA_TOPK, nb)
    scale = dh ** -0.5
    if ksel > 0:
        kb = k[:, :nb * MOBA_BLOCK].reshape(n, nb, MOBA_BLOCK, h, dh)
        vb = v[:, :nb * MOBA_BLOCK].reshape(n, nb, MOBA_BLOCK, h, dh)
        kmean = jnp.mean(kb.astype(F32), axis=2)
        n_i = jnp.arange(n)[:, None, None, None]
        h_i = jnp.arange(h)[None, None, :, None]
    qc = math.gcd(nq, MOBA_QCHUNK)

    def chunk(args):
        qq, pp = args
        own = pp // MOBA_BLOCK
        kpos = (own * MOBA_BLOCK)[:, None] + jnp.arange(MOBA_BLOCK)[None, :]
        own_ok = kpos <= pp[:, None]
        kpos_c = jnp.minimum(kpos, L - 1)
        k_own = k[:, kpos_c]
        v_own = v[:, kpos_c]
        s_own = jnp.einsum('nqhd,nqkhd->nqhk', qq, k_own, preferred_element_type=F32) * scale
        s_own = jnp.where(own_ok[None, :, None, :], s_own, -jnp.inf)
        if ksel > 0:
            gs = jnp.einsum('nqhd,nbhd->nqhb', qq.astype(F32), kmean)
            past_ok = jnp.arange(nb)[None, :] < own[:, None]
            gs = jnp.where(past_ok[None, :, None, :], gs, -jnp.inf)
            _, idx = lax.top_k(gs, ksel)
            sel_ok = idx < own[None, :, None, None]
            k_sel = kb[n_i, idx, :, h_i]
            v_sel = vb[n_i, idx, :, h_i]
            s_sel = jnp.einsum('nqhd,nqhjkd->nqhjk', qq, k_sel, preferred_element_type=F32) * scale
            s_sel = jnp.where(sel_ok[..., None], s_sel, -jnp.inf)
            s = jnp.concatenate([s_sel.reshape(n, qc, h, ksel * MOBA_BLOCK), s_own], axis=-1)
            p = jax.nn.softmax(s, axis=-1)
            o = jnp.einsum('nqhk,nqkhd->nqhd', p[..., ksel * MOBA_BLOCK:], v_own)
            o = o + jnp.einsum('nqhjk,nqhjkd->nqhd',
                               p[..., :ksel * MOBA_BLOCK].reshape(n, qc, h, ksel, MOBA_BLOCK), v_sel)
        else:
            p = jax.nn.softmax(s_own, axis=-1)
            o = jnp.einsum('nqhk,nqkhd->nqhd', p, v_own)
        return o.astype(q.dtype)

    qs = q.reshape(n, nq // qc, qc, h, dh).transpose(1, 0, 2, 3, 4)
    out = lax.map(chunk, (qs, q_pos.reshape(nq // qc, qc)))
    return out.transpose(1, 0, 2, 3, 4).reshape(n, nq, h, dh)


def diff_qkv(u, pos, qn, kn):
    n, t, _ = u.shape
    q, k, v, gate = jnp.split(u, 4, axis=-1)
    q = rope(rms_norm(q.reshape(n, t, B_HEADS, 2, HEAD_DIM), qn), pos)
    k = rope(rms_norm(k.reshape(n, t, B_HEADS, 2, HEAD_DIM), kn), pos)
    v = v.reshape(n, t, B_HEADS, B_VDIM)
    return q, k, v, gate


def diff_attend(q, k, v, q_pos, lam):
    n, nq, h, _, dh = q.shape
    L = k.shape[1]
    kpos = jnp.arange(L)
    qc = math.gcd(nq, ATTN_QBLOCK)

    def block(args):
        qq, pp = args
        s = jnp.einsum('nqhce,nkhce->nchqk', qq, k, preferred_element_type=F32) * dh ** -0.5
        s = jnp.where((kpos[None, :] <= pp[:, None])[None, None, None], s, -jnp.inf)
        a = jax.nn.softmax(s, axis=-1)
        w = a[:, 0] - lam * a[:, 1]
        return jnp.einsum('nhqk,nkhe->nqhe', w, v).astype(q.dtype)

    qs = q.reshape(n, nq // qc, qc, h, 2, dh).transpose(1, 0, 2, 3, 4, 5)
    o = lax.map(block, (qs, q_pos.reshape(nq // qc, qc)))
    return o.transpose(1, 0, 2, 3, 4).reshape(n, nq, h, v.shape[-1])


def dilated_qkv(u, pos, qn, kn):
    n, t, _ = u.shape
    parts = u[..., :3 * C_GROUPS * C_WIDTH].reshape(n, t, C_GROUPS, 3, C_HEADS, HEAD_DIM)
    gate = u[..., 3 * C_GROUPS * C_WIDTH:]
    q = rope(rms_norm(parts[:, :, :, 0], qn[:, None, :]), pos)
    k = rope(rms_norm(parts[:, :, :, 1], kn[:, None, :]), pos)
    v = parts[:, :, :, 2]
    return q, k, v, gate


def dilated_prompt(q, k, v, window, dilation):
    n, S, h, dh = q.shape
    J = window // dilation
    U = -(-S // dilation)
    nb = -(-U // J)
    pad = nb * J * dilation - S

    def to_res(x):
        x = jnp.pad(x, ((0, 0), (0, pad), (0, 0), (0, 0)))
        x = x.reshape(n, nb * J, dilation, h, dh).transpose(0, 2, 1, 3, 4)
        return x.reshape(n, dilation, nb, J, h, dh)

    def with_prev(x):
        prev = jnp.pad(x, ((0, 0), (0, 0), (1, 0), (0, 0), (0, 0), (0, 0)))[:, :, :-1]
        return jnp.concatenate([prev, x], axis=3)

    qr = to_res(q)
    kr = with_prev(to_res(k))
    vr = with_prev(to_res(v))
    s = jnp.einsum('nrcqhe,nrckhe->nrchqk', qr, kr, preferred_element_type=F32) * dh ** -0.5
    uq = jnp.arange(J)[:, None] + J
    uk = jnp.arange(2 * J)[None, :]
    dist = uq - uk
    exists = (jnp.arange(nb)[:, None, None] * J + uk[None] - J) >= 0
    ok = ((dist >= 0) & (dist <= J))[None] & exists
    s = jnp.where(ok[None, None, :, None], s, -jnp.inf)
    lse = jax.nn.logsumexp(s, axis=-1)
    p = jnp.exp(s - lse[..., None])
    o = jnp.einsum('nrchqk,nrckhe->nrcqhe', p, vr)
    o = o.reshape(n, dilation, nb * J, h, dh).transpose(0, 2, 1, 3, 4).reshape(n, nb * J * dilation, h, dh)[:, :S]
    lse = lse.transpose(0, 1, 2, 4, 3).reshape(n, dilation, nb * J, h).transpose(0, 2, 1, 3)
    lse = lse.reshape(n, nb * J * dilation, h)[:, :S]
    return o, lse


def dilated_sample(q, k_ext, v_ext, n_buf, window, dilation):
    n, nq, h, dh = q.shape
    J = window // dilation
    idx = (n_buf + jnp.arange(nq))[:, None] - dilation * jnp.arange(J + 1)[None, :]
    ok = idx >= 0
    idx = jnp.maximum(idx, 0)
    kg = k_ext[:, idx]
    vg = v_ext[:, idx]
    s = jnp.einsum('nqhe,nqjhe->nqhj', q, kg, preferred_element_type=F32) * dh ** -0.5
    s = jnp.where(ok[None, :, None, :], s, -jnp.inf)
    lse = jax.nn.logsumexp(s, axis=-1)
    p = jnp.exp(s - lse[..., None])
    o = jnp.einsum('nqhj,nqjhe->nqhe', p, vg)
    return o, lse


def combine_by_denominator(outs, lses):
    wgt = jax.nn.softmax(jnp.stack(lses, axis=0), axis=0)
    return jnp.sum(wgt[..., None] * jnp.stack(outs, axis=0).astype(F32), axis=0)


def setup_inputs(seed: int = 0) -> dict:
    key = jax.random.key(seed)
    keys = jax.random.split(key, 32)
    n_pages = PAST_LEN // PAGE_SIZE
    n_used = DEC_BATCH * n_pages
    n_pool = n_used + (n_used + 3) // 4
    sd = D_MODEL ** -0.5

    def nrm(i, shape, scale=1.0):
        return scale * jax.random.normal(keys[i], shape, F32)

    def gain(i, shape):
        return 1.0 + 0.02 * jax.random.normal(keys[i], shape, F32)

    page_table = jax.random.permutation(keys[9], n_pool)[:n_used].reshape(DEC_BATCH, n_pages).astype(jnp.int32)
    c_len = [min(w, PAST_LEN) for w, _ in C_PATTERNS]
    return {
        'x_prompt': nrm(0, (BATCH, SEQ, D_MODEL)),
        'x_sample': nrm(1, (DEC_BATCH, DEC_SEQ, D_MODEL)),
        'cache_a_k': nrm(2, (N_A, n_pool, PAGE_SIZE, A_HEADS, HEAD_DIM)),
        'cache_a_v': nrm(3, (N_A, n_pool, PAGE_SIZE, A_HEADS, HEAD_DIM)),
        'cache_b_k': nrm(4, (N_B, n_pool, PAGE_SIZE, B_HEADS, 2 * HEAD_DIM)),
        'cache_b_v': nrm(5, (N_B, n_pool, PAGE_SIZE, B_HEADS, B_VDIM)),
        'cache_c_kv0': nrm(6, (N_C, DEC_BATCH, c_len[0], 2, C_HEADS, HEAD_DIM)),
        'cache_c_kv1': nrm(7, (N_C, DEC_BATCH, c_len[1], 2, C_HEADS, HEAD_DIM)),
        'cache_c_kv2': nrm(8, (N_C, DEC_BATCH, c_len[2], 2, C_HEADS, HEAD_DIM)),
        'page_table': page_table,
        'norm_a': gain(10, (N_A, D_MODEL)),
        'w_in_a': nrm(11, (N_A, D_MODEL, 4 * A_WIDTH), sd),
        'qn_a': gain(12, (N_A, HEAD_DIM)),
        'kn_a': gain(13, (N_A, HEAD_DIM)),
        'w_out_a': nrm(14, (N_A, A_WIDTH, D_MODEL), A_WIDTH ** -0.5),
        'norm_b': gain(15, (N_B, D_MODEL)),
        'w_in_b': nrm(16, (N_B, D_MODEL, 4 * B_WIDTH), sd),
        'qn_b': gain(17, (N_B, HEAD_DIM)),
        'kn_b': gain(18, (N_B, HEAD_DIM)),
        'lam_q1_b': nrm(19, (N_B, HEAD_DIM), 0.1),
        'lam_k1_b': nrm(20, (N_B, HEAD_DIM), 0.1),
        'lam_q2_b': nrm(21, (N_B, HEAD_DIM), 0.1),
        'lam_k2_b': nrm(22, (N_B, HEAD_DIM), 0.1),
        'subln_b': gain(23, (N_B, B_VDIM)),
        'w_out_b': nrm(24, (N_B, B_WIDTH, D_MODEL), B_WIDTH ** -0.5),
        'norm_c': gain(25, (N_C, D_MODEL)),
        'w_in_c': nrm(26, (N_C, D_MODEL, (3 * C_GROUPS + 1) * C_WIDTH), sd),
        'qn_c': gain(27, (N_C, C_GROUPS, HEAD_DIM)),
        'kn_c': gain(28, (N_C, C_GROUPS, HEAD_DIM)),
        'w_out_c': nrm(29, (N_C, C_WIDTH, D_MODEL), C_WIDTH ** -0.5),
    }


def reference(x_prompt, x_sample, cache_a_k, cache_a_v, cache_b_k, cache_b_v,
              cache_c_kv0, cache_c_kv1, cache_c_kv2, page_table,
              norm_a, w_in_a, qn_a, kn_a, w_out_a,
              norm_b, w_in_b, qn_b, kn_b, lam_q1_b, lam_k1_b, lam_q2_b, lam_k2_b, subln_b, w_out_b,
              norm_c, w_in_c, qn_c, kn_c, w_out_c):
    S = x_prompt.shape[1]
    n_dec, DS = x_sample.shape[0], x_sample.shape[1]
    past = page_table.shape[1] * cache_a_k.shape[2]
    pos_p = jnp.arange(S, dtype=jnp.int32)
    pos_s = past + jnp.arange(DS, dtype=jnp.int32)
    c_caches = (cache_c_kv0, cache_c_kv1, cache_c_kv2)
    hp, hs = x_prompt, x_sample
    ak_p, av_p, ak_s, av_s = [], [], [], []
    bk_p, bv_p, bk_s, bv_s = [], [], [], []
    ckv_p = [[] for _ in C_PATTERNS]
    ckv_s = [[] for _ in C_PATTERNS]
    for layer in range(DEPTH):
        kind, j = layer % N_MIXERS, layer // N_MIXERS
        if kind == 0:
            up = project_in(hp, norm_a[j], w_in_a[j])
            us = project_in(hs, norm_a[j], w_in_a[j])
            qp, kp, vp, gp = moba_qkv(up, pos_p, qn_a[j], kn_a[j])
            qs, ks, vs, gs = moba_qkv(us, pos_s, qn_a[j], kn_a[j])
            op = moba_attend(qp, kp, vp, pos_p)
            k_all = jnp.concatenate([gather_pages(cache_a_k[j], page_table).astype(ks.dtype), ks], axis=1)
            v_all = jnp.concatenate([gather_pages(cache_a_v[j], page_table).astype(vs.dtype), vs], axis=1)
            os_ = moba_attend(qs, k_all, v_all, pos_s)
            hp = hp + gated_out(op, gp, w_out_a[j]).astype(hp.dtype)
            hs = hs + gated_out(os_, gs, w_out_a[j]).astype(hs.dtype)
            ak_p.append(kp)
            av_p.append(vp)
            ak_s.append(ks)
            av_s.append(vs)
        elif kind == 1:
            lam_init = 0.8 - 0.6 * math.exp(-0.3 * layer)
            lam = (jnp.exp(jnp.sum(lam_q1_b[j].astype(F32) * lam_k1_b[j].astype(F32)))
                   - jnp.exp(jnp.sum(lam_q2_b[j].astype(F32) * lam_k2_b[j].astype(F32))) + lam_init)
            up = project_in(hp, norm_b[j], w_in_b[j])
            us = project_in(hs, norm_b[j], w_in_b[j])
            qp, kp, vp, gp = diff_qkv(up, pos_p, qn_b[j], kn_b[j])
            qs, ks, vs, gs = diff_qkv(us, pos_s, qn_b[j], kn_b[j])
            op = diff_attend(qp, kp, vp, pos_p, lam)
            k_past = gather_pages(cache_b_k[j], page_table).reshape(n_dec, past, B_HEADS, 2, HEAD_DIM)
            k_all = jnp.concatenate([k_past.astype(ks.dtype), ks], axis=1)
            v_all = jnp.concatenate([gather_pages(cache_b_v[j], page_table).astype(vs.dtype), vs], axis=1)
            os_ = diff_attend(qs, k_all, v_all, pos_s, lam)
            op = rms_norm(op, subln_b[j]) * (1.0 - lam_init)
            os_ = rms_norm(os_, subln_b[j]) * (1.0 - lam_init)
            hp = hp + gated_out(op, gp, w_out_b[j]).astype(hp.dtype)
            hs = hs + gated_out(os_, gs, w_out_b[j]).astype(hs.dtype)
            bk_p.append(kp.reshape(kp.shape[0], S, B_HEADS, 2 * HEAD_DIM))
            bv_p.append(vp)
            bk_s.append(ks.reshape(n_dec, DS, B_HEADS, 2 * HEAD_DIM))
            bv_s.append(vs)
        else:
            up = project_in(hp, norm_c[j], w_in_c[j])
            us = project_in(hs, norm_c[j], w_in_c[j])
            qp, kp, vp, gp = dilated_qkv(up, pos_p, qn_c[j], kn_c[j])
            qs, ks, vs, gs = dilated_qkv(us, pos_s, qn_c[j], kn_c[j])
            outs_p, lses_p, outs_s, lses_s = [], [], [], []
            for g, (win, dil) in enumerate(C_PATTERNS):
                o, l = dilated_prompt(qp[:, :, g], kp[:, :, g], vp[:, :, g], win, dil)
                outs_p.append(o)
                lses_p.append(l)
                buf = c_caches[g][j]
                n_buf = buf.shape[1]
                k_ext = jnp.concatenate([buf[:, :, 0].astype(ks.dtype), ks[:, :, g]], axis=1)
                v_ext = jnp.concatenate([buf[:, :, 1].astype(vs.dtype), vs[:, :, g]], axis=1)
                o2, l2 = dilated_sample(qs[:, :, g], k_ext, v_ext, n_buf, win, dil)
                outs_s.append(o2)
                lses_s.append(l2)
                keep_p = min(win, S)
                keep_s = min(win, n_buf + DS)
                ckv_p[g].append(jnp.stack([kp[:, S - keep_p:, g], vp[:, S - keep_p:, g]], axis=2))
                ckv_s[g].append(jnp.stack([k_ext[:, n_buf + DS - keep_s:], v_ext[:, n_buf + DS - keep_s:]], axis=2))
            op = combine_by_denominator(outs_p, lses_p)
            os_ = combine_by_denominator(outs_s, lses_s)
            hp = hp + gated_out(op, gp, w_out_c[j]).astype(hp.dtype)
            hs = hs + gated_out(os_, gs, w_out_c[j]).astype(hs.dtype)
    y_prompt, y_sample = hp, hs
    a_k_prompt, a_v_prompt = jnp.stack(ak_p), jnp.stack(av_p)
    a_k_sample, a_v_sample = jnp.stack(ak_s), jnp.stack(av_s)
    b_k_prompt, b_v_prompt = jnp.stack(bk_p), jnp.stack(bv_p)
    b_k_sample, b_v_sample = jnp.stack(bk_s), jnp.stack(bv_s)
    c_kv0_prompt, c_kv1_prompt, c_kv2_prompt = jnp.stack(ckv_p[0]), jnp.stack(ckv_p[1]), jnp.stack(ckv_p[2])
    c_kv0_sample, c_kv1_sample, c_kv2_sample = jnp.stack(ckv_s[0]), jnp.stack(ckv_s[1]), jnp.stack(ckv_s[2])
    return (y_prompt, y_sample, a_k_prompt, a_v_prompt, a_k_sample, a_v_sample,
            b_k_prompt, b_v_prompt, b_k_sample, b_v_sample,
            c_kv0_prompt, c_kv1_prompt, c_kv2_prompt, c_kv0_sample, c_kv1_sample, c_kv2_sample)
```

```python
import functools
import math

import jax
import jax.numpy as jnp
from jax import lax
from jax.experimental import pallas as pl
from jax.experimental.pallas import tpu as pltpu

F32 = jnp.float32
BF16 = jnp.bfloat16

D_MODEL = 1024
HEAD_DIM = 64
LANES = 128
COL_TILE = 1024
EPS = 1e-6
ROPE_THETA = 10000.0
SCALE = HEAD_DIM ** -0.5
NEG = -1e30
MOBA_BLOCK = 256
MOBA_TOPK = 3
C_PATTERNS = ((128, 1), (512, 4), (2048, 16))
VMEM_LIMIT = 48 * 1024 * 1024

_ABT = (((1,), (1,)), ((), ()))


def _params(n_axes):
    return pltpu.CompilerParams(dimension_semantics=("arbitrary",) * n_axes,
                                vmem_limit_bytes=VMEM_LIMIT)


def _lane_lo():
    return lax.broadcasted_iota(jnp.int32, (1, LANES), 1) < HEAD_DIM


def _row_picks_head():
    row = lax.broadcasted_iota(jnp.int32, (8, LANES), 0)
    lane = lax.broadcasted_iota(jnp.int32, (8, LANES), 1)
    return (lane // HEAD_DIM) == row


def _proj_body(mode_ref, x_ref, g_ref, w_ref, gain_ref, cos_ref, sin_ref, bd_ref, o_ref, xn_ref):
    j = pl.program_id(1)

    @pl.when(j == 0)
    def _():
        x = x_ref[...]
        ms = jnp.mean(x * x, axis=-1, keepdims=True)
        xn_ref[...] = (x * lax.rsqrt(ms + EPS) * g_ref[...]).astype(BF16)

    u = jnp.dot(xn_ref[...], w_ref[...], preferred_element_type=F32)
    mode = mode_ref[j]

    @pl.when(mode == 0)
    def _():
        o_ref[...] = u

    @pl.when(mode == 1)
    def _():
        gain = gain_ref[...]
        cos = cos_ref[...]
        sin = sin_ref[...]
        lane = lax.broadcasted_iota(jnp.int32, (1, LANES), 1)
        first_half = (lane % HEAD_DIM) < (HEAD_DIM // 2)
        for c in range(COL_TILE // 256):
            uc = u[:, c * 256:(c + 1) * 256]
            ss = jnp.dot((uc * uc).astype(BF16), bd_ref[...], preferred_element_type=F32)
            y = uc * lax.rsqrt(ss * (1.0 / HEAD_DIM) + EPS) * gain[:, c * 256:(c + 1) * 256]
            for s in range(2):
                ys = y[:, s * LANES:(s + 1) * LANES]
                rot = jnp.where(first_half, pltpu.roll(ys, 96, 1), pltpu.roll(ys, 32, 1))
                o_ref[:, c * 256 + s * LANES:c * 256 + (s + 1) * LANES] = ys * cos + rot * sin


def _project(x, g, w_bf, gains, modes, cos, sin, bd, tm):
    t, e = x.shape[0], w_bf.shape[1]
    nt = e // COL_TILE
    npb = cos.shape[0] // tm
    grid_spec = pltpu.PrefetchScalarGridSpec(
        num_scalar_prefetch=1,
        grid=(t // tm, nt),
        in_specs=[
            pl.BlockSpec((tm, D_MODEL), lambda i, j, m: (i, 0)),
            pl.BlockSpec((1, D_MODEL), lambda i, j, m: (0, 0)),
            pl.BlockSpec((D_MODEL, COL_TILE), lambda i, j, m: (0, j)),
            pl.BlockSpec((None, 1, COL_TILE), lambda i, j, m: (j, 0, 0)),
            pl.BlockSpec((tm, LANES), lambda i, j, m: (i % npb, 0)),
            pl.BlockSpec((tm, LANES), lambda i, j, m: (i % npb, 0)),
            pl.BlockSpec((256, 256), lambda i, j, m: (0, 0)),
        ],
        out_specs=pl.BlockSpec((tm, COL_TILE), lambda i, j, m: (i, j)),
        scratch_shapes=[pltpu.VMEM((tm, D_MODEL), BF16)],
    )
    return pl.pallas_call(
        _proj_body, grid_spec=grid_spec,
        out_shape=jax.ShapeDtypeStruct((t, e), F32),
        compiler_params=_params(2), name="proj_in",
    )(modes, x, g.reshape(1, D_MODEL), w_bf, gains, cos, sin, bd)


def _outproj_body(o_ref, gate_ref, h_ref, w_ref, y_ref):
    g = gate_ref[...]
    z = o_ref[...] * (g * jax.nn.sigmoid(g))
    y_ref[...] = h_ref[...] + jnp.dot(z.astype(BF16), w_ref[...], preferred_element_type=F32)


def _out_project(o, u, h, w_bf, tm):
    t = o.shape[0]
    gate_blk = u.shape[1] // COL_TILE - 1
    return pl.pallas_call(
        _outproj_body, grid=(t // tm,),
        in_specs=[
            pl.BlockSpec((tm, D_MODEL), lambda i: (i, 0)),
            pl.BlockSpec((tm, COL_TILE), lambda i: (i, gate_blk)),
            pl.BlockSpec((tm, D_MODEL), lambda i: (i, 0)),
            pl.BlockSpec((D_MODEL, D_MODEL), lambda i: (0, 0)),
        ],
        out_specs=pl.BlockSpec((tm, D_MODEL), lambda i: (i, 0)),
        out_shape=jax.ShapeDtypeStruct((t, D_MODEL), F32),
        compiler_params=_params(1), name="proj_out",
    )(o, u, h, w_bf)


def _flash_tile(qm, kblk, vblk, ok, m_ref, l_ref, acc_ref, h):
    s = lax.dot_general(qm, kblk, _ABT, preferred_element_type=F32)
    if ok is not None:
        s = jnp.where(ok, s, NEG)
    m_old = m_ref[h]
    m_new = jnp.maximum(m_old, jnp.max(s, axis=1, keepdims=True))
    alpha = jnp.exp(m_old - m_new)
    p = jnp.exp(s - m_new)
    l_ref[h] = alpha * l_ref[h] + jnp.sum(p, axis=1, keepdims=True)
    acc_ref[h] = alpha * acc_ref[h] + jnp.dot(p.astype(BF16), vblk, preferred_element_type=F32)
    m_ref[h] = m_new


def _flash_point(s_new, v_new, m_ref, l_ref, acc_ref, h):
    m_old = m_ref[h]
    m_new = jnp.maximum(m_old, s_new)
    alpha = jnp.exp(m_old - m_new)
    p = jnp.exp(s_new - m_new)
    l_ref[h] = alpha * l_ref[h] + p
    acc_ref[h] = alpha * acc_ref[h] + p * v_new
    m_ref[h] = m_new


def _flash_init(m_ref, l_ref, acc_ref):
    m_ref[...] = jnp.full(m_ref.shape, NEG, F32)
    l_ref[...] = jnp.zeros(l_ref.shape, F32)
    acc_ref[...] = jnp.zeros(acc_ref.shape, F32)


def _moba_prompt_body(q_ref, k_ref, v_ref, o_ref, kb_ref, vb_ref, m_ref, l_ref, acc_ref, *, seq, tq):
    nb = seq // MOBA_BLOCK
    lo = _lane_lo()
    kb_ref[...] = k_ref[...].astype(BF16)
    vb_ref[...] = v_ref[...].astype(BF16)
    kmean = jnp.concatenate(
        [jnp.mean(k_ref[b * MOBA_BLOCK:(b + 1) * MOBA_BLOCK, :], axis=0, keepdims=True) for b in range(nb)],
        axis=0)
    kcat = jnp.concatenate([jnp.where(lo, kmean, 0.0), jnp.where(lo, 0.0, kmean)], axis=0)

    def qtile(qi, carry):
        r0 = pl.multiple_of(qi * tq, tq)
        q = q_ref[pl.ds(r0, tq), :]
        own = (qi * tq) // MOBA_BLOCK
        gate = lax.dot_general(q, kcat, _ABT, precision=lax.Precision.HIGHEST,
                               preferred_element_type=F32)
        qs = q * SCALE
        qm = (jnp.where(lo, qs, 0.0).astype(BF16), jnp.where(lo, 0.0, qs).astype(BF16))
        bl = lax.broadcasted_iota(jnp.int32, (tq, nb), 1)
        ownv = jnp.full((tq, 1), own, jnp.int32)
        sels = []
        for h in range(2):
            gh = gate[:, h * nb:(h + 1) * nb]
            cnt = jnp.zeros((tq, nb), F32)
            for b2 in range(nb):
                col = gh[:, b2:b2 + 1]
                beats = jnp.where(col > gh, 1.0, jnp.where(col == gh, jnp.where(bl > b2, 1.0, 0.0), 0.0))
                cnt = cnt + jnp.where(ownv > b2, beats, 0.0)
            sels.append(jnp.where(bl < ownv, jnp.where(cnt < MOBA_TOPK, 1.0, 0.0), 0.0))
        _flash_init(m_ref, l_ref, acc_ref)
        rowpos = r0 + lax.broadcasted_iota(jnp.int32, (tq, 1), 0)
        for kb in range(nb):
            ks = slice(kb * MOBA_BLOCK, (kb + 1) * MOBA_BLOCK)

            @pl.when(kb < own)
            def _():
                for h in range(2):
                    ok = sels[h][:, kb:kb + 1] > 0.5
                    _flash_tile(qm[h], kb_ref[ks, :], vb_ref[ks, :], ok, m_ref, l_ref, acc_ref, h)

            @pl.when(kb == own)
            def _():
                colpos = kb * MOBA_BLOCK + lax.broadcasted_iota(jnp.int32, (1, MOBA_BLOCK), 1)
                ok = colpos <= rowpos
                for h in range(2):
                    _flash_tile(qm[h], kb_ref[ks, :], vb_ref[ks, :], ok, m_ref, l_ref, acc_ref, h)

        o_ref[pl.ds(r0, tq), :] = jnp.where(lo, acc_ref[0] / l_ref[0], acc_ref[1] / l_ref[1])
        return carry

    lax.fori_loop(0, seq // tq, qtile, 0)


def _moba_prompt(u3, tq=128):
    n, seq, _ = u3.shape
    nslab = D_MODEL // LANES
    body = functools.partial(_moba_prompt_body, seq=seq, tq=tq)
    return pl.pallas_call(
        body, grid=(n, nslab),
        in_specs=[
            pl.BlockSpec((None, seq, LANES), lambda b, s: (b, 0, s)),
            pl.BlockSpec((None, seq, LANES), lambda b, s: (b, 0, nslab + s)),
            pl.BlockSpec((None, seq, LANES), lambda b, s: (b, 0, 2 * nslab + s)),
        ],
        out_specs=pl.BlockSpec((None, seq, LANES), lambda b, s: (b, 0, s)),
        out_shape=jax.ShapeDtypeStruct((n, seq, D_MODEL), F32),
        scratch_shapes=[
            pltpu.VMEM((seq, LANES), BF16), pltpu.VMEM((seq, LANES), BF16),
            pltpu.VMEM((2, tq, 1), F32), pltpu.VMEM((2, tq, 1), F32), pltpu.VMEM((2, tq, LANES), F32),
        ],
        compiler_params=_params(2), name="moba_prompt",
    )(u3, u3, u3)


def _lambda_value(lq1_ref, lk1_ref, lq2_ref, lk2_ref, lam_init):
    a = jnp.sum(lq1_ref[...] * lk1_ref[...], axis=1, keepdims=True)
    b = jnp.sum(lq2_ref[...] * lk2_ref[...], axis=1, keepdims=True)
    return jnp.exp(a) - jnp.exp(b) + lam_init


def _diff_finish(a1, a2, lam, subln, lam_init):
    o = a1 - lam * a2
    ms = jnp.mean(o * o, axis=-1, keepdims=True)
    return (o * lax.rsqrt(ms + EPS) * subln) * (1.0 - lam_init)


def _diff_prompt_body(q_ref, k_ref, v_ref, lq1_ref, lk1_ref, lq2_ref, lk2_ref, subln_ref, o_ref,
                      kb_ref, vb_ref, m_ref, l_ref, acc_ref, *, seq, tq, tk, lam_init):
    lo = _lane_lo()
    kb_ref[...] = k_ref[...].astype(BF16)
    vb_ref[...] = v_ref[...].astype(BF16)
    lam = _lambda_value(lq1_ref, lk1_ref, lq2_ref, lk2_ref, lam_init)
    subln = subln_ref[...]

    def qtile(qi, carry):
        r0 = pl.multiple_of(qi * tq, tq)
        qs = q_ref[pl.ds(r0, tq), :] * SCALE
        qm = (jnp.where(lo, qs, 0.0).astype(BF16), jnp.where(lo, 0.0, qs).astype(BF16))
        own = (qi * tq) // tk
        _flash_init(m_ref, l_ref, acc_ref)
        rowpos = r0 + lax.broadcasted_iota(jnp.int32, (tq, 1), 0)
        for kb in range(seq // tk):
            ks = slice(kb * tk, (kb + 1) * tk)

            @pl.when(kb < own)
            def _():
                for c in range(2):
                    _flash_tile(qm[c], kb_ref[ks, :], vb_ref[ks, :], None, m_ref, l_ref, acc_ref, c)

            @pl.when(kb == own)
            def _():
                colpos = kb * tk + lax.broadcasted_iota(jnp.int32, (1, tk), 1)
                ok = colpos <= rowpos
                for c in range(2):
                    _flash_tile(qm[c], kb_ref[ks, :], vb_ref[ks, :], ok, m_ref, l_ref, acc_ref, c)

        o_ref[pl.ds(r0, tq), :] = _diff_finish(acc_ref[0] / l_ref[0], acc_ref[1] / l_ref[1], lam, subln, lam_init)
        return carry

    lax.fori_loop(0, seq // tq, qtile, 0)


def _diff_prompt(u3, lams, subln, lam_init, tq=128, tk=256):
    n, seq, _ = u3.shape
    nslab = D_MODEL // LANES
    body = functools.partial(_diff_prompt_body, seq=seq, tq=tq, tk=tk, lam_init=lam_init)
    vec = pl.BlockSpec((1, HEAD_DIM), lambda b, s: (0, 0))
    return pl.pallas_call(
        body, grid=(n, nslab),
        in_specs=[
            pl.BlockSpec((None, seq, LANES), lambda b, s: (b, 0, s)),
            pl.BlockSpec((None, seq, LANES), lambda b, s: (b, 0, nslab + s)),
            pl.BlockSpec((None, seq, LANES), lambda b, s: (b, 0, 2 * nslab + s)),
            vec, vec, vec, vec,
            pl.BlockSpec((1, LANES), lambda b, s: (0, 0)),
        ],
        out_specs=pl.BlockSpec((None, seq, LANES), lambda b, s: (b, 0, s)),
        out_shape=jax.ShapeDtypeStruct((n, seq, D_MODEL), F32),
        scratch_shapes=[
            pltpu.VMEM((seq, LANES), BF16), pltpu.VMEM((seq, LANES), BF16),
            pltpu.VMEM((2, tq, 1), F32), pltpu.VMEM((2, tq, 1), F32), pltpu.VMEM((2, tq, LANES), F32),
        ],
        compiler_params=_params(2), name="diff_prompt",
    )(u3, u3, u3, *lams, subln)


def _dilated_prompt_body(*refs, seq):
    qkv_refs = refs[:9]
    o_ref = refs[9]
    qc_ref, kc_ref, vc_ref, ocm_ref, lcm_ref, og_ref, lg_ref = refs[10:]
    lo = _lane_lo()
    blk = 128
    nblk = seq // blk
    ri = lax.broadcasted_iota(jnp.int32, (blk, 2 * blk), 0)
    ci = lax.broadcasted_iota(jnp.int32, (blk, 2 * blk), 1)
    ok_band = (ci >= ri) & (ci <= ri + blk)
    ok_own = (lax.broadcasted_iota(jnp.int32, (blk, blk), 1) <= lax.broadcasted_iota(jnp.int32, (blk, blk), 0))

    def attend(qb, kk, vv, ok):
        outs, lses = [], []
        for h in range(2):
            qm = jnp.where(lo, qb, 0.0) if h == 0 else jnp.where(lo, 0.0, qb)
            s = lax.dot_general(qm.astype(BF16), kk, _ABT, preferred_element_type=F32)
            s = jnp.where(ok, s, NEG)
            m = jnp.max(s, axis=1, keepdims=True)
            p = jnp.exp(s - m)
            l = jnp.sum(p, axis=1, keepdims=True)
            outs.append(jnp.dot(p.astype(BF16), vv, preferred_element_type=F32) / l)
            lses.append(m + jnp.log(l))
        return jnp.where(lo, outs[0], outs[1]), jnp.where(lo, lses[0], lses[1])

    for g, (win, dil) in enumerate(C_PATTERNS):
        q_ref, k_ref, v_ref = qkv_refs[3 * g:3 * g + 3]
        ncls = seq // dil
        bpc = ncls // blk
        for r in range(dil):
            rows = pl.ds(r, ncls, stride=dil) if dil > 1 else slice(None)
            dst = slice(r * ncls, (r + 1) * ncls)
            qc_ref[dst, :] = q_ref[rows, :] * SCALE
            kc_ref[dst, :] = k_ref[rows, :].astype(BF16)
            vc_ref[dst, :] = v_ref[rows, :].astype(BF16)

        def block(gb, carry):
            r0 = pl.multiple_of(gb * blk, blk)
            qb = qc_ref[pl.ds(r0, blk), :]
            has_prev = (gb % bpc) != 0

            @pl.when(has_prev)
            def _():
                p0 = pl.multiple_of(gb * blk - blk, blk)
                o, lse = attend(qb, kc_ref[pl.ds(p0, 2 * blk), :], vc_ref[pl.ds(p0, 2 * blk), :], ok_band)
                ocm_ref[pl.ds(r0, blk), :] = o
                lcm_ref[pl.ds(r0, blk), :] = lse

            @pl.when(jnp.logical_not(has_prev))
            def _():
                o, lse = attend(qb, kc_ref[pl.ds(r0, blk), :], vc_ref[pl.ds(r0, blk), :], ok_own)
                ocm_ref[pl.ds(r0, blk), :] = o
                lcm_ref[pl.ds(r0, blk), :] = lse

            return carry

        lax.fori_loop(0, nblk, block, 0)
        for r in range(dil):
            rows = pl.ds(r, ncls, stride=dil) if dil > 1 else slice(None)
            src = slice(r * ncls, (r + 1) * ncls)
            og_ref[g, rows, :] = ocm_ref[src, :]
            lg_ref[g, rows, :] = lcm_ref[src, :]

    lmax = jnp.maximum(jnp.maximum(lg_ref[0], lg_ref[1]), lg_ref[2])
    w = [jnp.exp(lg_ref[g] - lmax) for g in range(3)]
    den = w[0] + w[1] + w[2]
    o_ref[...] = (w[0] * og_ref[0] + w[1] * og_ref[1] + w[2] * og_ref[2]) / den


def _dilated_prompt(u3):
    n, seq, _ = u3.shape
    nslab = D_MODEL // LANES
    in_specs = []
    for g in range(3):
        for c in range(3):
            base = (3 * g + c) * nslab
            in_specs.append(pl.BlockSpec((None, seq, LANES), lambda b, s, base=base: (b, 0, base + s)))
    body = functools.partial(_dilated_prompt_body, seq=seq)
    return pl.pallas_call(
        body, grid=(n, nslab),
        in_specs=in_specs,
        out_specs=pl.BlockSpec((None, seq, LANES), lambda b, s: (b, 0, s)),
        out_shape=jax.ShapeDtypeStruct((n, seq, D_MODEL), F32),
        scratch_shapes=[
            pltpu.VMEM((seq, LANES), F32), pltpu.VMEM((seq, LANES), BF16), pltpu.VMEM((seq, LANES), BF16),
            pltpu.VMEM((seq, LANES), F32), pltpu.VMEM((seq, LANES), F32),
            pltpu.VMEM((3, seq, LANES), F32), pltpu.VMEM((3, seq, LANES), F32),
        ],
        compiler_params=_params(2), name="dilated_prompt",
    )(*([u3] * 9))


def _kmean_body(pt_ref, k_ref, o_ref, *, pages_per_block, page):
    pg = pl.program_id(1) % pages_per_block
    s = jnp.sum(k_ref[...], axis=0, keepdims=True)

    @pl.when(pg == 0)
    def _():
        o_ref[...] = s

    @pl.when(pg > 0)
    def _():
        o_ref[...] = o_ref[...] + s

    @pl.when(pg == pages_per_block - 1)
    def _():
        o_ref[...] = o_ref[...] * (1.0 / (pages_per_block * page))


def _moba_kmean(pool_k, page_table):
    n_dec, n_pages = page_table.shape
    page = pool_k.shape[1]
    ppb = MOBA_BLOCK // page
    grid_spec = pltpu.PrefetchScalarGridSpec(
        num_scalar_prefetch=1, grid=(n_dec, n_pages),
        in_specs=[pl.BlockSpec((None, page, D_MODEL), lambda b, p, pt: (pt[b, p], 0, 0))],
        out_specs=pl.BlockSpec((None, None, 1, D_MODEL), lambda b, p, pt: (b, p // ppb, 0, 0)),
    )
    out = pl.pallas_call(
        functools.partial(_kmean_body, pages_per_block=ppb, page=page), grid_spec=grid_spec,
        out_shape=jax.ShapeDtypeStruct((n_dec, n_pages // ppb, 1, D_MODEL), F32),
        compiler_params=_params(2), name="moba_kmean",
    )(page_table, pool_k)
    return out.reshape(n_dec, n_pages // ppb, D_MODEL)


def _moba_select_body(km_ref, q_ref, seg_ref, o_ref):
    nbk = km_ref.shape[0]
    gate = jnp.dot(km_ref[...] * q_ref[...], seg_ref[...], precision=lax.Precision.HIGHEST,
                   preferred_element_type=F32)
    rows = lax.broadcasted_iota(jnp.int32, (nbk, LANES), 0)
    o_ref[...] = jnp.zeros(o_ref.shape, jnp.int32)
    for j in range(MOBA_TOPK):
        mx = jnp.max(gate, axis=0, keepdims=True)
        idx = jnp.min(jnp.where(gate == mx, rows, nbk), axis=0, keepdims=True)
        o_ref[j:j + 1, :] = idx
        gate = jnp.where(rows == idx, -jnp.inf, gate)


def _moba_select(kmean, q3, seg):
    n_dec, nbk, _ = kmean.shape
    return pl.pallas_call(
        _moba_select_body, grid=(n_dec,),
        in_specs=[
            pl.BlockSpec((None, nbk, D_MODEL), lambda b: (b, 0, 0)),
            pl.BlockSpec((None, 1, D_MODEL), lambda b: (b, 0, 0)),
            pl.BlockSpec((D_MODEL, LANES), lambda b: (0, 0)),
        ],
        out_specs=pl.BlockSpec((None, 8, LANES), lambda b: (b, 0, 0)),
        out_shape=jax.ShapeDtypeStruct((n_dec, 8, LANES), jnp.int32),
        compiler_params=_params(1), name="moba_select",
    )(kmean, q3, seg)


def _moba_sample_body(pt_ref, idx_ref, q_ref, kn_ref, vn_ref, k_ref, v_ref, o_ref, m_ref, l_ref, acc_ref, *, n_steps):
    h = pl.program_id(1)
    jp = pl.program_id(2)
    lane = lax.broadcasted_iota(jnp.int32, (1, LANES), 1)
    mine = (lane // HEAD_DIM) == (h % 2)

    @pl.when(jp == 0)
    def _():
        _flash_init(m_ref, l_ref, acc_ref)

    qm = jnp.broadcast_to(jnp.where(mine, q_ref[...] * SCALE, 0.0), (8, LANES))
    _flash_tile(qm.astype(BF16), k_ref[...].astype(BF16), v_ref[...].astype(BF16), None, m_ref, l_ref, acc_ref, 0)

    @pl.when(jp == n_steps - 1)
    def _():
        s_self = jnp.sum(qm * kn_ref[...], axis=1, keepdims=True)
        _flash_point(s_self, vn_ref[...], m_ref, l_ref, acc_ref, 0)
        o_ref[...] = (acc_ref[0] / l_ref[0])[0:1, :]


def _moba_sample(pool_k, pool_v, page_table, sel, us3, n_heads):
    n_dec = page_table.shape[0]
    page = pool_k.shape[1]
    ppb = MOBA_BLOCK // page
    n_steps = MOBA_TOPK * ppb
    nslab = D_MODEL // LANES

    def page_of(b, h, jp, pt, idx):
        return pt[b, ppb * idx[b, (jp // ppb) * n_heads + h] + jp % ppb]

    grid_spec = pltpu.PrefetchScalarGridSpec(
        num_scalar_prefetch=2, grid=(n_dec, n_heads, n_steps),
        in_specs=[
            pl.BlockSpec((None, 1, LANES), lambda b, h, jp, pt, idx: (b, 0, h // 2)),
            pl.BlockSpec((None, 1, LANES), lambda b, h, jp, pt, idx: (b, 0, nslab + h // 2)),
            pl.BlockSpec((None, 1, LANES), lambda b, h, jp, pt, idx: (b, 0, 2 * nslab + h // 2)),
            pl.BlockSpec((None, page, LANES), lambda b, h, jp, pt, idx: (page_of(b, h, jp, pt, idx), 0, h // 2)),
            pl.BlockSpec((None, page, LANES), lambda b, h, jp, pt, idx: (page_of(b, h, jp, pt, idx), 0, h // 2)),
        ],
        out_specs=pl.BlockSpec((None, None, 1, LANES), lambda b, h, jp, pt, idx: (b, h, 0, 0)),
        scratch_shapes=[pltpu.VMEM((1, 8, 1), F32), pltpu.VMEM((1, 8, 1), F32), pltpu.VMEM((1, 8, LANES), F32)],
    )
    out = pl.pallas_call(
        functools.partial(_moba_sample_body, n_steps=n_steps), grid_spec=grid_spec,
        out_shape=jax.ShapeDtypeStruct((n_dec, n_heads, 1, LANES), F32),
        compiler_params=_params(3), name="moba_sample",
    )(page_table, sel, us3, us3, us3, pool_k, pool_v)
    out = out.reshape(n_dec, n_heads // 2, 2, 2, HEAD_DIM)
    return jnp.stack([out[:, :, 0, 0], out[:, :, 1, 1]], axis=2).reshape(n_dec, D_MODEL)


def _diff_sample_body(pt_ref, q_ref, kn_ref, vn_ref, lq1_ref, lk1_ref, lq2_ref, lk2_ref, subln_ref,
                      k_ref, v_ref, o_ref, m_ref, l_ref, acc_ref, *, n_heads, page, lam_init):
    pg = pl.program_id(1)
    pick = _row_picks_head()

    @pl.when(pg == 0)
    def _():
        _flash_init(m_ref, l_ref, acc_ref)

    qms = []
    for h in range(n_heads):
        qm = jnp.where(pick, q_ref[h:h + 1, :] * SCALE, 0.0)
        qms.append(qm)
        kh = k_ref[pl.ds(h, page, stride=n_heads), :].astype(BF16)
        vh = v_ref[pl.ds(h, page, stride=n_heads), :].astype(BF16)
        _flash_tile(qm.astype(BF16), kh, vh, None, m_ref, l_ref, acc_ref, h)

    @pl.when(pg == pl.num_programs(1) - 1)
    def _():
        lam = _lambda_value(lq1_ref, lk1_ref, lq2_ref, lk2_ref, lam_init)
        for h in range(n_heads):
            s_self = jnp.sum(qms[h] * kn_ref[h:h + 1, :], axis=1, keepdims=True)
            _flash_point(s_self, vn_ref[h:h + 1, :], m_ref, l_ref, acc_ref, h)
            a = acc_ref[h] / l_ref[h]
            o_ref[h:h + 1, :] = _diff_finish(a[0:1, :], a[1:2, :], lam, subln_ref[...], lam_init)


def _diff_sample(pool_k, pool_v, page_table, us4, lams, subln, lam_init):
    n_dec, n_pages = page_table.shape
    n_heads = us4.shape[2]
    rows = pool_k.shape[1]
    vec = pl.BlockSpec((1, HEAD_DIM), lambda b, p, pt: (0, 0))
    grid_spec = pltpu.PrefetchScalarGridSpec(
        num_scalar_prefetch=1, grid=(n_dec, n_pages),
        in_specs=[
            pl.BlockSpec((None, None, n_heads, LANES), lambda b, p, pt: (b, 0, 0, 0)),
            pl.BlockSpec((None, None, n_heads, LANES), lambda b, p, pt: (b, 1, 0, 0)),
            pl.BlockSpec((None, None, n_heads, LANES), lambda b, p, pt: (b, 2, 0, 0)),
            vec, vec, vec, vec,
            pl.BlockSpec((1, LANES), lambda b, p, pt: (0, 0)),
            pl.BlockSpec((None, rows, LANES), lambda b, p, pt: (pt[b, p], 0, 0)),
            pl.BlockSpec((None, rows, LANES), lambda b, p, pt: (pt[b, p], 0, 0)),
        ],
        out_specs=pl.BlockSpec((None, n_heads, LANES), lambda b, p, pt: (b, 0, 0)),
        scratch_shapes=[pltpu.VMEM((n_heads, 8, 1), F32), pltpu.VMEM((n_heads, 8, 1), F32),
                        pltpu.VMEM((n_heads, 8, LANES), F32)],
    )
    body = functools.partial(_diff_sample_body, n_heads=n_heads, page=rows // n_heads, lam_init=lam_init)
    out = pl.pallas_call(
        body, grid_spec=grid_spec,
        out_shape=jax.ShapeDtypeStruct((n_dec, n_heads, LANES), F32),
        compiler_params=_params(2), name="diff_sample",
    )(page_table, us4, us4, us4, *lams, subln, pool_k, pool_v)
    return out.reshape(n_dec, D_MODEL)


def _dilated_sample_body(u_ref, c0_ref, c1_ref, c2_ref, o_ref):
    lo = _lane_lo()
    pick = _row_picks_head()
    nslab = D_MODEL // LANES
    for s in range(nslab):
        ms, ls, accs = [], [], []
        for g, c_ref in enumerate((c0_ref, c1_ref, c2_ref)):
            base = 3 * g * D_MODEL + s * LANES
            q = u_ref[:, base:base + LANES] * SCALE
            kn = u_ref[:, base + D_MODEL:base + D_MODEL + LANES]
            vn = u_ref[:, base + 2 * D_MODEL:base + 2 * D_MODEL + LANES]
            qm = jnp.where(pick, q, 0.0)
            kk = c_ref[:, s * LANES:(s + 1) * LANES].astype(BF16)
            vv = c_ref[:, D_MODEL + s * LANES:D_MODEL + (s + 1) * LANES].astype(BF16)
            sc = lax.dot_general(qm.astype(BF16), kk, _ABT, preferred_element_type=F32)
            s_self = jnp.sum(qm * kn, axis=1, keepdims=True)
            m = jnp.maximum(jnp.max(sc, axis=1, keepdims=True), s_self)
            p = jnp.exp(sc - m)
            p_self = jnp.exp(s_self - m)
            ms.append(m)
            ls.append(jnp.sum(p, axis=1, keepdims=True) + p_self)
            accs.append(jnp.dot(p.astype(BF16), vv, preferred_element_type=F32) + p_self * vn)
        mx = jnp.maximum(jnp.maximum(ms[0], ms[1]), ms[2])
        w = [jnp.exp(ms[g] - mx) for g in range(3)]
        den = w[0] * ls[0] + w[1] * ls[1] + w[2] * ls[2]
        o = (w[0] * accs[0] + w[1] * accs[1] + w[2] * accs[2]) / den
        o_ref[:, s * LANES:(s + 1) * LANES] = jnp.where(lo, o[0:1, :], o[1:2, :])


def _dilated_sample(us3, caches):
    n_dec = us3.shape[0]
    e = us3.shape[2]
    in_specs = [pl.BlockSpec((None, 1, e), lambda b: (b, 0, 0))]
    for c in caches:
        in_specs.append(pl.BlockSpec((None, c.shape[1], 2 * D_MODEL), lambda b: (b, 0, 0)))
    out = pl.pallas_call(
        _dilated_sample_body, grid=(n_dec,),
        in_specs=in_specs,
        out_specs=pl.BlockSpec((None, 1, D_MODEL), lambda b: (b, 0, 0)),
        out_shape=jax.ShapeDtypeStruct((n_dec, 1, D_MODEL), F32),
        compiler_params=_params(1), name="dilated_sample",
    )(us3, *caches)
    return out.reshape(n_dec, D_MODEL)


def _rope_tables(pos):
    half = HEAD_DIM // 2
    inv = ROPE_THETA ** (-jnp.arange(half, dtype=F32) / half)
    ang = pos.astype(F32)[:, None] * inv[None, :]
    cos, sin = jnp.cos(ang), jnp.sin(ang)
    cos = jnp.concatenate([cos, cos], axis=1)
    sin = jnp.concatenate([-sin, sin], axis=1)
    return jnp.tile(cos, (1, LANES // HEAD_DIM)), jnp.tile(sin, (1, LANES // HEAD_DIM))


def _head_gain(g):
    return jnp.tile(g.astype(F32), D_MODEL // HEAD_DIM)


def kernel(x_prompt, x_sample, cache_a_k, cache_a_v, cache_b_k, cache_b_v, cache_c_kv0, cache_c_kv1, cache_c_kv2,
           page_table, norm_a, w_in_a, qn_a, kn_a, w_out_a, norm_b, w_in_b, qn_b, kn_b, lam_q1_b, lam_k1_b,
           lam_q2_b, lam_k2_b, subln_b, w_out_b, norm_c, w_in_c, qn_c, kn_c, w_out_c):
    n, seq, _ = x_prompt.shape
    n_dec, dec_seq, _ = x_sample.shape
    assert dec_seq == 1 and seq % 512 == 0
    page = cache_a_k.shape[2]
    n_pages = page_table.shape[1]
    past = n_pages * page
    n_pool = cache_a_k.shape[1]
    depth = norm_a.shape[0] + norm_b.shape[0] + norm_c.shape[0]
    tm = 512
    a_heads = D_MODEL // HEAD_DIM
    b_heads = D_MODEL // (2 * HEAD_DIM)
    c_caches = (cache_c_kv0, cache_c_kv1, cache_c_kv2)

    cos_p, sin_p = _rope_tables(jnp.arange(seq, dtype=jnp.int32))
    cos_s, sin_s = _rope_tables(jnp.full((n_dec,), past, jnp.int32))
    bd = (jnp.arange(256)[:, None] // HEAD_DIM == jnp.arange(256)[None, :] // HEAD_DIM).astype(BF16)
    seg = (jnp.arange(D_MODEL)[:, None] // HEAD_DIM == jnp.arange(LANES)[None, :]).astype(F32)
    zero_gain = jnp.zeros((D_MODEL,), F32)

    hp = x_prompt.reshape(n * seq, D_MODEL)
    hs = x_sample.reshape(n_dec, D_MODEL)
    ak_p, av_p, ak_s, av_s = [], [], [], []
    bk_p, bv_p, bk_s, bv_s = [], [], [], []
    ckv_p = [[] for _ in C_PATTERNS]
    ckv_s = [[] for _ in C_PATTERNS]

    def project_both(norm, w_in, gains, modes):
        w_bf = w_in.astype(BF16)
        gains = gains.reshape(gains.shape[0], 1, COL_TILE)
        up = _project(hp, norm, w_bf, gains, modes, cos_p, sin_p, bd, tm)
        us = _project(hs, norm, w_bf, gains, modes, cos_s, sin_s, bd, n_dec)
        return up, us

    for layer in range(depth):
        kind, j = layer % 3, layer // 3
        if kind == 0:
            gains = jnp.stack([_head_gain(qn_a[j]), _head_gain(kn_a[j]), zero_gain, zero_gain])
            modes = jnp.array([1, 1, 0, 0], jnp.int32)
            up, us = project_both(norm_a[j], w_in_a[j], gains, modes)
            op = _moba_prompt(up.reshape(n, seq, 4 * D_MODEL)).reshape(n * seq, D_MODEL)
            pool_k = cache_a_k[j].reshape(n_pool, page, D_MODEL)
            pool_v = cache_a_v[j].reshape(n_pool, page, D_MODEL)
            us3 = us.reshape(n_dec, 1, 4 * D_MODEL)
            kmean = _moba_kmean(pool_k, page_table)
            sel = _moba_select(kmean, us3, seg)[:, :MOBA_TOPK, :a_heads].reshape(n_dec, MOBA_TOPK * a_heads)
            os_ = _moba_sample(pool_k, pool_v, page_table, sel, us3, a_heads)
            hp = _out_project(op, up, hp, w_out_a[j].astype(BF16), tm)
            hs = _out_project(os_, us, hs, w_out_a[j].astype(BF16), n_dec)
            ak_p.append(up[:, D_MODEL:2 * D_MODEL].reshape(n, seq, a_heads, HEAD_DIM))
            av_p.append(up[:, 2 * D_MODEL:3 * D_MODEL].reshape(n, seq, a_heads, HEAD_DIM))
            ak_s.append(us[:, D_MODEL:2 * D_MODEL].reshape(n_dec, 1, a_heads, HEAD_DIM))
            av_s.append(us[:, 2 * D_MODEL:3 * D_MODEL].reshape(n_dec, 1, a_heads, HEAD_DIM))
        elif kind == 1:
            lam_init = 0.8 - 0.6 * math.exp(-0.3 * layer)
            gains = jnp.stack([_head_gain(qn_b[j]), _head_gain(kn_b[j]), zero_gain, zero_gain])
            modes = jnp.array([1, 1, 0, 0], jnp.int32)
            up, us = project_both(norm_b[j], w_in_b[j], gains, modes)
            lams = tuple(v[j].astype(F32).reshape(1, HEAD_DIM) for v in (lam_q1_b, lam_k1_b, lam_q2_b, lam_k2_b))
            subln = subln_b[j].astype(F32).reshape(1, LANES)
            op = _diff_prompt(up.reshape(n, seq, 4 * D_MODEL), lams, subln, lam_init).reshape(n * seq, D_MODEL)
            pool_k = cache_b_k[j].reshape(n_pool, page * b_heads, LANES)
            pool_v = cache_b_v[j].reshape(n_pool, page * b_heads, LANES)
            os_ = _diff_sample(pool_k, pool_v, page_table, us.reshape(n_dec, 4, b_heads, LANES), lams, subln, lam_init)
            hp = _out_project(op, up, hp, w_out_b[j].astype(BF16), tm)
            hs = _out_project(os_, us, hs, w_out_b[j].astype(BF16), n_dec)
            bk_p.append(up[:, D_MODEL:2 * D_MODEL].reshape(n, seq, b_heads, LANES))
            bv_p.append(up[:, 2 * D_MODEL:3 * D_MODEL].reshape(n, seq, b_heads, LANES))
            bk_s.append(us[:, D_MODEL:2 * D_MODEL].reshape(n_dec, 1, b_heads, LANES))
            bv_s.append(us[:, 2 * D_MODEL:3 * D_MODEL].reshape(n_dec, 1, b_heads, LANES))
        else:
            gain_rows, mode_vals = [], []
            for g in range(len(C_PATTERNS)):
                gain_rows += [_head_gain(qn_c[j, g]), _head_gain(kn_c[j, g]), zero_gain]
                mode_vals += [1, 1, 0]
            gains = jnp.stack(gain_rows + [zero_gain])
            modes = jnp.array(mode_vals + [0], jnp.int32)
            up, us = project_both(norm_c[j], w_in_c[j], gains, modes)
            e = up.shape[1]
            op = _dilated_prompt(up.reshape(n, seq, e)).reshape(n * seq, D_MODEL)
            views = []
            for g, (win, dil) in enumerate(C_PATTERNS):
                buf = c_caches[g][j]
                n_buf = buf.shape[1]
                assert n_buf == win and win // dil == 128
                views.append(buf.reshape(n_dec, n_buf // dil, dil * 2 * D_MODEL))
            os_ = _dilated_sample(us.reshape(n_dec, 1, e), views)
            hp = _out_project(op, up, hp, w_out_c[j].astype(BF16), tm)
            hs = _out_project(os_, us, hs, w_out_c[j].astype(BF16), n_dec)
            up3 = up.reshape(n, seq, e)
            for g, (win, dil) in enumerate(C_PATTERNS):
                c0 = (3 * g + 1) * D_MODEL
                keep_p = min(win, seq)
                ckv_p[g].append(up3[:, seq - keep_p:, c0:c0 + 2 * D_MODEL].reshape(n, keep_p, 2, a_heads, HEAD_DIM))
                buf = c_caches[g][j]
                new = us[:, c0:c0 + 2 * D_MODEL].reshape(n_dec, 1, 2, a_heads, HEAD_DIM)
                ckv_s[g].append(jnp.concatenate([buf[:, 1:], new], axis=1))

    return (hp.reshape(n, seq, D_MODEL), hs.reshape(n_dec, 1, D_MODEL),
            jnp.stack(ak_p), jnp.stack(av_p), jnp.stack(ak_s), jnp.stack(av_s),
            jnp.stack(bk_p), jnp.stack(bv_p), jnp.stack(bk_s), jnp.stack(bv_s),
            jnp.stack(ckv_p[0]), jnp.stack(ckv_p[1]), jnp.stack(ckv_p[2]),
            jnp.stack(ckv_s[0]), jnp.stack(ckv_s[1]), jnp.stack(ckv_s[2]))
```

```python
import functools
import math

import jax
import jax.numpy as jnp
from jax import lax
from jax.experimental import pallas as pl
from jax.experimental.pallas import tpu as pltpu

F32 = jnp.float32
BF16 = jnp.bfloat16

D_MODEL = 1024
HEAD_DIM = 64
LANES = 128
COL_TILE = 1024
EPS = 1e-6
ROPE_THETA = 10000.0
SCALE = HEAD_DIM ** -0.5
NEG = -1e30
MOBA_BLOCK = 256
MOBA_TOPK = 3
C_PATTERNS = ((128, 1), (512, 4), (2048, 16))
VMEM_LIMIT = 48 * 1024 * 1024

_ABT = (((1,), (1,)), ((), ()))


def _params(n_axes):
    return pltpu.CompilerParams(dimension_semantics=("arbitrary",) * n_axes,
                                vmem_limit_bytes=VMEM_LIMIT)


def _lane_lo():
    return lax.broadcasted_iota(jnp.int32, (1, LANES), 1) < HEAD_DIM


def _row_picks_head():
    row = lax.broadcasted_iota(jnp.int32, (8, LANES), 0)
    lane = lax.broadcasted_iota(jnp.int32, (8, LANES), 1)
    return (lane // HEAD_DIM) == row


def _proj_body(mode_ref, x_ref, g_ref, w_ref, gain_ref, cos_ref, sin_ref, bd_ref, o_ref, xn_ref):
    j = pl.program_id(1)

    @pl.when(j == 0)
    def _():
        x = x_ref[...]
        ms = jnp.mean(x * x, axis=-1, keepdims=True)
        xn_ref[...] = (x * lax.rsqrt(ms + EPS) * g_ref[...]).astype(BF16)

    u = jnp.dot(xn_ref[...], w_ref[...], preferred_element_type=F32)
    mode = mode_ref[j]

    @pl.when(mode == 0)
    def _():
        o_ref[...] = u

    @pl.when(mode == 1)
    def _():
        gain = gain_ref[...]
        cos = cos_ref[...]
        sin = sin_ref[...]
        lane = lax.broadcasted_iota(jnp.int32, (1, LANES), 1)
        first_half = (lane % HEAD_DIM) < (HEAD_DIM // 2)
        for c in range(COL_TILE // 256):
            uc = u[:, c * 256:(c + 1) * 256]
            ss = jnp.dot((uc * uc).astype(BF16), bd_ref[...], preferred_element_type=F32)
            y = uc * lax.rsqrt(ss * (1.0 / HEAD_DIM) + EPS) * gain[:, c * 256:(c + 1) * 256]
            for s in range(2):
                ys = y[:, s * LANES:(s + 1) * LANES]
                rot = jnp.where(first_half, pltpu.roll(ys, 96, 1), pltpu.roll(ys, 32, 1))
                o_ref[:, c * 256 + s * LANES:c * 256 + (s + 1) * LANES] = ys * cos + rot * sin


def _project(x, g, w_bf, gains, modes, cos, sin, bd, tm):
    t, e = x.shape[0], w_bf.shape[1]
    nt = e // COL_TILE
    npb = cos.shape[0] // tm
    grid_spec = pltpu.PrefetchScalarGridSpec(
        num_scalar_prefetch=1,
        grid=(t // tm, nt),
        in_specs=[
            pl.BlockSpec((tm, D_MODEL), lambda i, j, m: (i, 0)),
            pl.BlockSpec((1, D_MODEL), lambda i, j, m: (0, 0)),
            pl.BlockSpec((D_MODEL, COL_TILE), lambda i, j, m: (0, j)),
            pl.BlockSpec((None, 1, COL_TILE), lambda i, j, m: (j, 0, 0)),
            pl.BlockSpec((tm, LANES), lambda i, j, m: (i % npb, 0)),
            pl.BlockSpec((tm, LANES), lambda i, j, m: (i % npb, 0)),
            pl.BlockSpec((256, 256), lambda i, j, m: (0, 0)),
        ],
        out_specs=pl.BlockSpec((tm, COL_TILE), lambda i, j, m: (i, j)),
        scratch_shapes=[pltpu.VMEM((tm, D_MODEL), BF16)],
    )
    return pl.pallas_call(
        _proj_body, grid_spec=grid_spec,
        out_shape=jax.ShapeDtypeStruct((t, e), F32),
        compiler_params=_params(2), name="proj_in",
    )(modes, x, g.reshape(1, D_MODEL), w_bf, gains, cos, sin, bd)


def _outproj_body(o_ref, gate_ref, h_ref, w_ref, y_ref):
    g = gate_ref[...]
    z = o_ref[...] * (g * jax.nn.sigmoid(g))
    y_ref[...] = h_ref[...] + jnp.dot(z.astype(BF16), w_ref[...], preferred_element_type=F32)


def _out_project(o, u, h, w_bf, tm):
    t = o.shape[0]
    gate_blk = u.shape[1] // COL_TILE - 1
    return pl.pallas_call(
        _outproj_body, grid=(t // tm,),
        in_specs=[
            pl.BlockSpec((tm, D_MODEL), lambda i: (i, 0)),
            pl.BlockSpec((tm, COL_TILE), lambda i: (i, gate_blk)),
            pl.BlockSpec((tm, D_MODEL), lambda i: (i, 0)),
            pl.BlockSpec((D_MODEL, D_MODEL), lambda i: (0, 0)),
        ],
        out_specs=pl.BlockSpec((tm, D_MODEL), lambda i: (i, 0)),
        out_shape=jax.ShapeDtypeStruct((t, D_MODEL), F32),
        compiler_params=_params(1), name="proj_out",
    )(o, u, h, w_bf)


def _two_pass_rows(qm, qi, rowpos, past_ok, kb_ref, vb_ref, s_ref, mx_ref, ls_ref, acc_ref, *, nkb, tk):
    mx_ref[...] = jnp.full(mx_ref.shape, NEG, F32)
    ls_ref[...] = jnp.zeros(ls_ref.shape, F32)
    acc_ref[...] = jnp.zeros(acc_ref.shape, F32)
    for kb in range(nkb):
        ks = slice(kb * tk, (kb + 1) * tk)

        @pl.when(kb < qi)
        def _():
            for h in range(2):
                s = lax.dot_general(qm[h], kb_ref[ks, :], _ABT, preferred_element_type=F32)
                ok = past_ok(h, kb)
                if ok is not None:
                    s = jnp.where(ok, s, NEG)
                s_ref[h, kb] = s
                mx_ref[h] = jnp.maximum(mx_ref[h], s)

        @pl.when(kb == qi)
        def _():
            colpos = kb * tk + lax.broadcasted_iota(jnp.int32, (1, tk), 1)
            ok = colpos <= rowpos
            for h in range(2):
                s = lax.dot_general(qm[h], kb_ref[ks, :], _ABT, preferred_element_type=F32)
                s = jnp.where(ok, s, NEG)
                s_ref[h, kb] = s
                mx_ref[h] = jnp.maximum(mx_ref[h], s)

    m = [jnp.max(mx_ref[h], axis=1, keepdims=True) for h in range(2)]
    for kb in range(nkb):
        ks = slice(kb * tk, (kb + 1) * tk)

        @pl.when(kb <= qi)
        def _():
            for h in range(2):
                p = jnp.exp(s_ref[h, kb] - m[h])
                ls_ref[h] = ls_ref[h] + p
                acc_ref[h] = acc_ref[h] + jnp.dot(p.astype(BF16), vb_ref[ks, :], preferred_element_type=F32)

    return [acc_ref[h] / jnp.sum(ls_ref[h], axis=1, keepdims=True) for h in range(2)]


def _two_pass_scratch(seq, tq, tk):
    return [
        pltpu.VMEM((seq, LANES), BF16), pltpu.VMEM((seq, LANES), BF16),
        pltpu.VMEM((2, seq // tk, tq, tk), F32),
        pltpu.VMEM((2, tq, tk), F32), pltpu.VMEM((2, tq, tk), F32), pltpu.VMEM((2, tq, LANES), F32),
    ]


def _moba_prompt_body(q_ref, k_ref, v_ref, o_ref, kb_ref, vb_ref, s_ref, mx_ref, ls_ref, acc_ref, *, seq):
    tq = MOBA_BLOCK
    nb = seq // MOBA_BLOCK
    lo = _lane_lo()
    kb_ref[...] = k_ref[...].astype(BF16)
    vb_ref[...] = v_ref[...].astype(BF16)
    kmean = jnp.concatenate(
        [jnp.mean(k_ref[b * MOBA_BLOCK:(b + 1) * MOBA_BLOCK, :], axis=0, keepdims=True) for b in range(nb)],
        axis=0)
    kcat = jnp.concatenate([jnp.where(lo, kmean, 0.0), jnp.where(lo, 0.0, kmean)], axis=0)

    def qblock(qi, carry):
        r0 = pl.multiple_of(qi * tq, tq)
        q = q_ref[pl.ds(r0, tq), :]
        gate = lax.dot_general(q, kcat, _ABT, precision=lax.Precision.HIGHEST,
                               preferred_element_type=F32)
        qs = q * SCALE
        qm = (jnp.where(lo, qs, 0.0).astype(BF16), jnp.where(lo, 0.0, qs).astype(BF16))
        bl = lax.broadcasted_iota(jnp.int32, (tq, nb), 1)
        ownv = jnp.full((tq, 1), qi, jnp.int32)
        sels = []
        for h in range(2):
            gh = gate[:, h * nb:(h + 1) * nb]
            cnt = jnp.zeros((tq, nb), F32)
            for b2 in range(nb):
                col = gh[:, b2:b2 + 1]
                beats = jnp.where(col > gh, 1.0, jnp.where(col == gh, jnp.where(bl > b2, 1.0, 0.0), 0.0))
                cnt = cnt + jnp.where(ownv > b2, beats, 0.0)
            sels.append(jnp.where(bl < ownv, jnp.where(cnt < MOBA_TOPK, 1.0, 0.0), 0.0))
        rowpos = r0 + lax.broadcasted_iota(jnp.int32, (tq, 1), 0)
        outs = _two_pass_rows(qm, qi, rowpos, lambda h, kb: sels[h][:, kb:kb + 1] > 0.5,
                              kb_ref, vb_ref, s_ref, mx_ref, ls_ref, acc_ref, nkb=nb, tk=MOBA_BLOCK)
        o_ref[pl.ds(r0, tq), :] = jnp.where(lo, outs[0], outs[1])
        return carry

    lax.fori_loop(0, seq // tq, qblock, 0)


def _moba_prompt(u3):
    n, seq, _ = u3.shape
    nslab = D_MODEL // LANES
    body = functools.partial(_moba_prompt_body, seq=seq)
    return pl.pallas_call(
        body, grid=(n, nslab),
        in_specs=[
            pl.BlockSpec((None, seq, LANES), lambda b, s: (b, 0, s)),
            pl.BlockSpec((None, seq, LANES), lambda b, s: (b, 0, nslab + s)),
            pl.BlockSpec((None, seq, LANES), lambda b, s: (b, 0, 2 * nslab + s)),
        ],
        out_specs=pl.BlockSpec((None, seq, LANES), lambda b, s: (b, 0, s)),
        out_shape=jax.ShapeDtypeStruct((n, seq, D_MODEL), F32),
        scratch_shapes=_two_pass_scratch(seq, MOBA_BLOCK, MOBA_BLOCK),
        compiler_params=_params(2), name="moba_prompt",
    )(u3, u3, u3)


def _lambda_value(lq1_ref, lk1_ref, lq2_ref, lk2_ref, lam_init):
    a = jnp.sum(lq1_ref[...] * lk1_ref[...], axis=1, keepdims=True)
    b = jnp.sum(lq2_ref[...] * lk2_ref[...], axis=1, keepdims=True)
    return jnp.exp(a) - jnp.exp(b) + lam_init


def _diff_finish(a1, a2, lam, subln, lam_init):
    o = a1 - lam * a2
    ms = jnp.mean(o * o, axis=-1, keepdims=True)
    return (o * lax.rsqrt(ms + EPS) * subln) * (1.0 - lam_init)


def _diff_prompt_body(q_ref, k_ref, v_ref, lq1_ref, lk1_ref, lq2_ref, lk2_ref, subln_ref, o_ref,
                      kb_ref, vb_ref, s_ref, mx_ref, ls_ref, acc_ref, *, seq, tq, lam_init):
    lo = _lane_lo()
    kb_ref[...] = k_ref[...].astype(BF16)
    vb_ref[...] = v_ref[...].astype(BF16)
    lam = _lambda_value(lq1_ref, lk1_ref, lq2_ref, lk2_ref, lam_init)
    subln = subln_ref[...]

    def qblock(qi, carry):
        r0 = pl.multiple_of(qi * tq, tq)
        qs = q_ref[pl.ds(r0, tq), :] * SCALE
        qm = (jnp.where(lo, qs, 0.0).astype(BF16), jnp.where(lo, 0.0, qs).astype(BF16))
        rowpos = r0 + lax.broadcasted_iota(jnp.int32, (tq, 1), 0)
        outs = _two_pass_rows(qm, qi, rowpos, lambda h, kb: None,
                              kb_ref, vb_ref, s_ref, mx_ref, ls_ref, acc_ref, nkb=seq // tq, tk=tq)
        o_ref[pl.ds(r0, tq), :] = _diff_finish(outs[0], outs[1], lam, subln, lam_init)
        return carry

    lax.fori_loop(0, seq // tq, qblock, 0)


def _diff_prompt(u3, lams, subln, lam_init, tq=256):
    n, seq, _ = u3.shape
    nslab = D_MODEL // LANES
    body = functools.partial(_diff_prompt_body, seq=seq, tq=tq, lam_init=lam_init)
    vec = pl.BlockSpec((1, HEAD_DIM), lambda b, s: (0, 0))
    return pl.pallas_call(
        body, grid=(n, nslab),
        in_specs=[
            pl.BlockSpec((None, seq, LANES), lambda b, s: (b, 0, s)),
            pl.BlockSpec((None, seq, LANES), lambda b, s: (b, 0, nslab + s)),
            pl.BlockSpec((None, seq, LANES), lambda b, s: (b, 0, 2 * nslab + s)),
            vec, vec, vec, vec,
            pl.BlockSpec((1, LANES), lambda b, s: (0, 0)),
        ],
        out_specs=pl.BlockSpec((None, seq, LANES), lambda b, s: (b, 0, s)),
        out_shape=jax.ShapeDtypeStruct((n, seq, D_MODEL), F32),
        scratch_shapes=_two_pass_scratch(seq, tq, tq),
        compiler_params=_params(2), name="diff_prompt",
    )(u3, u3, u3, *lams, subln)


def _dilated_prompt_body(*refs, seq, unroll):
    qkv_refs = refs[:9]
    o_ref = refs[9]
    qc_ref, kc_ref, vc_ref, ocm_ref, lcm_ref, og_ref, lg_ref = refs[10:]
    lo = _lane_lo()
    blk = 128
    nblk = seq // blk
    ri = lax.broadcasted_iota(jnp.int32, (blk, 2 * blk), 0)
    ci = lax.broadcasted_iota(jnp.int32, (blk, 2 * blk), 1)
    bias_band = jnp.where((ci >= ri) & (ci <= ri + blk), 0.0, NEG)
    bias_own = jnp.where((ci >= blk) & (ci <= ri + blk), 0.0, NEG)
    kc_ref[0:blk, :] = jnp.zeros((blk, LANES), BF16)
    vc_ref[0:blk, :] = jnp.zeros((blk, LANES), BF16)

    def attend(qb, kk, vv, bias):
        outs, lses = [], []
        for h in range(2):
            qm = jnp.where(lo, qb, 0.0) if h == 0 else jnp.where(lo, 0.0, qb)
            s = lax.dot_general(qm.astype(BF16), kk, _ABT, preferred_element_type=F32) + bias
            m = jnp.max(s, axis=1, keepdims=True)
            p = jnp.exp(s - m)
            l = jnp.sum(p, axis=1, keepdims=True)
            outs.append(jnp.dot(p.astype(BF16), vv, preferred_element_type=F32) / l)
            lses.append(m + jnp.log(l))
        return jnp.where(lo, outs[0], outs[1]), jnp.where(lo, lses[0], lses[1])

    for g, (win, dil) in enumerate(C_PATTERNS):
        q_ref, k_ref, v_ref = qkv_refs[3 * g:3 * g + 3]
        ncls = seq // dil
        bpc = ncls // blk
        for r in range(dil):
            rows = pl.ds(r, ncls, stride=dil) if dil > 1 else slice(None)
            qc_ref[r * ncls:(r + 1) * ncls, :] = q_ref[rows, :] * SCALE
            kc_ref[blk + r * ncls:blk + (r + 1) * ncls, :] = k_ref[rows, :].astype(BF16)
            vc_ref[blk + r * ncls:blk + (r + 1) * ncls, :] = v_ref[rows, :].astype(BF16)

        def blocks(it, carry):
            for u in range(unroll):
                gb = it * unroll + u
                r0 = pl.multiple_of(gb * blk, blk)
                bias = jnp.where((gb % bpc) != 0, bias_band, bias_own)
                o, lse = attend(qc_ref[pl.ds(r0, blk), :], kc_ref[pl.ds(r0, 2 * blk), :],
                                vc_ref[pl.ds(r0, 2 * blk), :], bias)
                ocm_ref[pl.ds(r0, blk), :] = o
                lcm_ref[pl.ds(r0, blk), :] = lse
            return carry

        lax.fori_loop(0, nblk // unroll, blocks, 0)
        for r in range(dil):
            rows = pl.ds(r, ncls, stride=dil) if dil > 1 else slice(None)
            src = slice(r * ncls, (r + 1) * ncls)
            og_ref[g, rows, :] = ocm_ref[src, :]
            lg_ref[g, rows, :] = lcm_ref[src, :]

    lmax = jnp.maximum(jnp.maximum(lg_ref[0], lg_ref[1]), lg_ref[2])
    w = [jnp.exp(lg_ref[g] - lmax) for g in range(3)]
    den = w[0] + w[1] + w[2]
    o_ref[...] = (w[0] * og_ref[0] + w[1] * og_ref[1] + w[2] * og_ref[2]) / den


def _dilated_prompt(u3):
    n, seq, _ = u3.shape
    nslab = D_MODEL // LANES
    in_specs = []
    for g in range(3):
        for c in range(3):
            base = (3 * g + c) * nslab
            in_specs.append(pl.BlockSpec((None, seq, LANES), lambda b, s, base=base: (b, 0, base + s)))
    body = functools.partial(_dilated_prompt_body, seq=seq, unroll=4)
    return pl.pallas_call(
        body, grid=(n, nslab),
        in_specs=in_specs,
        out_specs=pl.BlockSpec((None, seq, LANES), lambda b, s: (b, 0, s)),
        out_shape=jax.ShapeDtypeStruct((n, seq, D_MODEL), F32),
        scratch_shapes=[
            pltpu.VMEM((seq, LANES), F32), pltpu.VMEM((seq + 128, LANES), BF16), pltpu.VMEM((seq + 128, LANES), BF16),
            pltpu.VMEM((seq, LANES), F32), pltpu.VMEM((seq, LANES), F32),
            pltpu.VMEM((3, seq, LANES), F32), pltpu.VMEM((3, seq, LANES), F32),
        ],
        compiler_params=_params(2), name="dilated_prompt",
    )(*([u3] * 9))


def _kmean_body(pt_ref, *refs, pages_per_block, page, pps):
    k_refs, o_ref = refs[:pps], refs[pps]
    for b in range(pps // pages_per_block):
        s = None
        for i in range(pages_per_block):
            t = jnp.sum(k_refs[b * pages_per_block + i][...], axis=0)
            s = t if s is None else s + t
        o_ref[b] = s * (1.0 / (pages_per_block * page))


def _moba_kmean(cache_k, layer, page_table, pps=4):
    n_dec, n_pages = page_table.shape
    _, _, page, heads, dim = cache_k.shape
    ppb = MOBA_BLOCK // page
    assert pps % ppb == 0 and n_pages % pps == 0
    pages = [pl.BlockSpec((None, None, page, heads, dim), lambda b, p, pt, i=i: (layer, pt[b, pps * p + i], 0, 0, 0))
             for i in range(pps)]
    grid_spec = pltpu.PrefetchScalarGridSpec(
        num_scalar_prefetch=1, grid=(n_dec, n_pages // pps),
        in_specs=pages,
        out_specs=pl.BlockSpec((None, pps // ppb, heads, dim), lambda b, p, pt: (b, p, 0, 0)),
    )
    return pl.pallas_call(
        functools.partial(_kmean_body, pages_per_block=ppb, page=page, pps=pps), grid_spec=grid_spec,
        out_shape=jax.ShapeDtypeStruct((n_dec, n_pages // ppb, heads, dim), F32),
        compiler_params=_params(2), name="moba_kmean",
    )(page_table, *([cache_k] * pps))


def _moba_select_body(km_ref, q_ref, o_ref):
    nbk = km_ref.shape[0]
    gate = jnp.sum(km_ref[...] * q_ref[...][None], axis=-1, keepdims=True)
    blk = lax.broadcasted_iota(jnp.int32, gate.shape, 0)
    for j in range(MOBA_TOPK):
        mx = jnp.max(gate, axis=0, keepdims=True)
        idx = jnp.min(jnp.where(gate == mx, blk, nbk), axis=0, keepdims=True)
        o_ref[j] = idx[0]
        gate = jnp.where(blk == idx, -jnp.inf, gate)


def _moba_select(kmean, q):
    n_dec, nbk, heads, dim = kmean.shape
    return pl.pallas_call(
        _moba_select_body, grid=(n_dec,),
        in_specs=[
            pl.BlockSpec((None, nbk, heads, dim), lambda b: (b, 0, 0, 0)),
            pl.BlockSpec((None, heads, dim), lambda b: (b, 0, 0)),
        ],
        out_specs=pl.BlockSpec((None, MOBA_TOPK, heads, 1), lambda b: (b, 0, 0, 0)),
        out_shape=jax.ShapeDtypeStruct((n_dec, MOBA_TOPK, heads, 1), jnp.int32),
        compiler_params=_params(1), name="moba_select",
    )(kmean, q)


def _moba_sample_body(pt_ref, sel_ref, q_ref, kn_ref, vn_ref, kpool, vpool, o_ref, kbuf, vbuf, sem,
                      *, layer, n_heads, page, ppb):
    b = pl.program_id(0)
    n_chunks = MOBA_TOPK * ppb
    slot = b % 2

    def copies(bb, sl):
        out = []
        for h in range(n_heads):
            for c in range(n_chunks):
                pg = pt_ref[bb, ppb * sel_ref[bb, (c // ppb) * n_heads + h] + c % ppb]
                dst = pl.ds(c * page, page)
                out.append(pltpu.make_async_copy(kpool.at[layer, pg, :, h, :], kbuf.at[sl, h, dst, :], sem.at[0, sl]))
                out.append(pltpu.make_async_copy(vpool.at[layer, pg, :, h, :], vbuf.at[sl, h, dst, :], sem.at[1, sl]))
        return out

    @pl.when(b == 0)
    def _():
        for cp in copies(b, slot):
            cp.start()

    @pl.when(b + 1 < pl.num_programs(0))
    def _():
        for cp in copies(b + 1, 1 - slot):
            cp.start()

    for cp in copies(b, slot):
        cp.wait()

    q = q_ref[...] * SCALE
    s_self = jnp.sum(q * kn_ref[...], axis=-1, keepdims=True)
    for h in range(n_heads):
        qh = jnp.broadcast_to(q[h:h + 1, :], (8, q.shape[1])).astype(BF16)
        s = lax.dot_general(qh, kbuf[slot, h].astype(BF16), _ABT, preferred_element_type=F32)
        ss = s_self[h:h + 1, :]
        m = jnp.maximum(jnp.max(s, axis=1, keepdims=True), ss)
        p = jnp.exp(s - m)
        ps = jnp.exp(ss - m)
        den = jnp.sum(p, axis=1, keepdims=True) + ps
        o = (jnp.dot(p.astype(BF16), vbuf[slot, h].astype(BF16), preferred_element_type=F32)
             + ps * vn_ref[h:h + 1, :]) / den
        o_ref[h:h + 1, :] = o[0:1, :]


def _moba_sample(cache_k, cache_v, layer, page_table, sel, q, kn, vn):
    n_dec = page_table.shape[0]
    _, _, page, heads, dim = cache_k.shape
    ppb = MOBA_BLOCK // page
    keys = MOBA_TOPK * MOBA_BLOCK
    row = pl.BlockSpec((None, heads, dim), lambda b, pt, idx: (b, 0, 0))
    grid_spec = pltpu.PrefetchScalarGridSpec(
        num_scalar_prefetch=2, grid=(n_dec,),
        in_specs=[row, row, row, pl.BlockSpec(memory_space=pl.ANY), pl.BlockSpec(memory_space=pl.ANY)],
        out_specs=row,
        scratch_shapes=[pltpu.VMEM((2, heads, keys, dim), F32), pltpu.VMEM((2, heads, keys, dim), F32),
                        pltpu.SemaphoreType.DMA((2, 2))],
    )
    out = pl.pallas_call(
        functools.partial(_moba_sample_body, layer=layer, n_heads=heads, page=page, ppb=ppb), grid_spec=grid_spec,
        out_shape=jax.ShapeDtypeStruct((n_dec, heads, dim), F32),
        compiler_params=_params(1), name="moba_sample",
    )(page_table, sel, q, kn, vn, cache_k, cache_v)
    return out.reshape(n_dec, heads * dim)


def _diff_sample_body(pt_ref, q_ref, kn_ref, vn_ref, lq1_ref, lk1_ref, lq2_ref, lk2_ref, subln_ref, *refs,
                      n_heads, page, pps, lam_init):
    k_refs, v_refs = refs[:pps], refs[pps:2 * pps]
    o_ref, m_ref, l_ref, acc_ref = refs[2 * pps:]
    pg = pl.program_id(1)
    nrow = 2 * n_heads
    row = lax.broadcasted_iota(jnp.int32, (nrow, LANES), 0)
    lane = lax.broadcasted_iota(jnp.int32, (nrow, LANES), 1)
    rowcol = lax.broadcasted_iota(jnp.int32, (nrow, 1), 0)

    @pl.when(pg == 0)
    def _():
        m_ref[...] = jnp.full(m_ref.shape, NEG, F32)
        l_ref[...] = jnp.zeros(l_ref.shape, F32)
        acc_ref[...] = jnp.zeros(acc_ref.shape, F32)

    lhs = [jnp.where(row == 2 * h + lane // HEAD_DIM, q_ref[h:h + 1, :] * SCALE, 0.0) for h in range(n_heads)]
    lhs_bf = [x.astype(BF16) for x in lhs]
    parts = []
    for i in range(pps):
        s = None
        for h in range(n_heads):
            kh = k_refs[i][pl.ds(h, page, stride=n_heads), :].astype(BF16)
            d = lax.dot_general(lhs_bf[h], kh, _ABT, preferred_element_type=F32)
            s = d if s is None else s + d
        parts.append(s)
    s = jnp.concatenate(parts, axis=1)
    m_old = m_ref[...]
    m_new = jnp.maximum(m_old, jnp.max(s, axis=1, keepdims=True))
    alpha = jnp.exp(m_old - m_new)
    p = jnp.exp(s - m_new)
    l_ref[...] = alpha * l_ref[...] + jnp.sum(p, axis=1, keepdims=True)
    m_ref[...] = m_new
    acc = alpha * acc_ref[...]
    for i in range(pps):
        pi = p[:, i * page:(i + 1) * page]
        for h in range(n_heads):
            ph = jnp.where(rowcol // 2 == h, pi, 0.0).astype(BF16)
            vh = v_refs[i][pl.ds(h, page, stride=n_heads), :].astype(BF16)
            acc = acc + jnp.dot(ph, vh, preferred_element_type=F32)
    acc_ref[...] = acc

    @pl.when(pg == pl.num_programs(1) - 1)
    def _():
        lam = _lambda_value(lq1_ref, lk1_ref, lq2_ref, lk2_ref, lam_init)
        q_all, kn_all, vn_all = lhs[0], None, None
        for h in range(1, n_heads):
            q_all = q_all + lhs[h]
        for h in range(n_heads):
            sel = rowcol // 2 == h
            kn_h = jnp.where(sel, kn_ref[h:h + 1, :], 0.0)
            vn_h = jnp.where(sel, vn_ref[h:h + 1, :], 0.0)
            kn_all = kn_h if kn_all is None else kn_all + kn_h
            vn_all = vn_h if vn_all is None else vn_all + vn_h
        s_self = jnp.sum(q_all * kn_all, axis=1, keepdims=True)
        m1 = m_ref[...]
        m2 = jnp.maximum(m1, s_self)
        a1 = jnp.exp(m1 - m2)
        p_self = jnp.exp(s_self - m2)
        a = (a1 * acc_ref[...] + p_self * vn_all) / (a1 * l_ref[...] + p_self)
        for h in range(n_heads):
            o_ref[h:h + 1, :] = _diff_finish(a[2 * h:2 * h + 1, :], a[2 * h + 1:2 * h + 2, :], lam,
                                             subln_ref[...], lam_init)


def _diff_sample(pool_k, pool_v, page_table, us4, lams, subln, lam_init, pps=4):
    n_dec, n_pages = page_table.shape
    n_heads = us4.shape[2]
    rows = pool_k.shape[1]
    assert n_pages % pps == 0
    vec = pl.BlockSpec((1, HEAD_DIM), lambda b, p, pt: (0, 0))
    pages = [pl.BlockSpec((None, rows, LANES), lambda b, p, pt, i=i: (pt[b, pps * p + i], 0, 0)) for i in range(pps)]
    grid_spec = pltpu.PrefetchScalarGridSpec(
        num_scalar_prefetch=1, grid=(n_dec, n_pages // pps),
        in_specs=[
            pl.BlockSpec((None, None, n_heads, LANES), lambda b, p, pt: (b, 0, 0, 0)),
            pl.BlockSpec((None, None, n_heads, LANES), lambda b, p, pt: (b, 1, 0, 0)),
            pl.BlockSpec((None, None, n_heads, LANES), lambda b, p, pt: (b, 2, 0, 0)),
            vec, vec, vec, vec,
            pl.BlockSpec((1, LANES), lambda b, p, pt: (0, 0)),
        ] + pages + pages,
        out_specs=pl.BlockSpec((None, n_heads, LANES), lambda b, p, pt: (b, 0, 0)),
        scratch_shapes=[pltpu.VMEM((2 * n_heads, 1), F32), pltpu.VMEM((2 * n_heads, 1), F32),
                        pltpu.VMEM((2 * n_heads, LANES), F32)],
    )
    body = functools.partial(_diff_sample_body, n_heads=n_heads, page=rows // n_heads, pps=pps, lam_init=lam_init)
    out = pl.pallas_call(
        body, grid_spec=grid_spec,
        out_shape=jax.ShapeDtypeStruct((n_dec, n_heads, LANES), F32),
        compiler_params=_params(2), name="diff_sample",
    )(page_table, us4, us4, us4, *lams, subln, *([pool_k] * pps), *([pool_v] * pps))
    return out.reshape(n_dec, D_MODEL)


def _dilated_sample_body(u_ref, c0_ref, c1_ref, c2_ref, o_ref):
    lo = _lane_lo()
    pick = _row_picks_head()
    nslab = D_MODEL // LANES
    for s in range(nslab):
        ms, ls, accs = [], [], []
        for g, c_ref in enumerate((c0_ref, c1_ref, c2_ref)):
            base = 3 * g * D_MODEL + s * LANES
            q = u_ref[:, base:base + LANES] * SCALE
            kn = u_ref[:, base + D_MODEL:base + D_MODEL + LANES]
            vn = u_ref[:, base + 2 * D_MODEL:base + 2 * D_MODEL + LANES]
            qm = jnp.where(pick, q, 0.0)
            kk = c_ref[:, s * LANES:(s + 1) * LANES].astype(BF16)
            vv = c_ref[:, D_MODEL + s * LANES:D_MODEL + (s + 1) * LANES].astype(BF16)
            sc = lax.dot_general(qm.astype(BF16), kk, _ABT, preferred_element_type=F32)
            s_self = jnp.sum(qm * kn, axis=1, keepdims=True)
            m = jnp.maximum(jnp.max(sc, axis=1, keepdims=True), s_self)
            p = jnp.exp(sc - m)
            p_self = jnp.exp(s_self - m)
            ms.append(m)
            ls.append(jnp.sum(p, axis=1, keepdims=True) + p_self)
            accs.append(jnp.dot(p.astype(BF16), vv, preferred_element_type=F32) + p_self * vn)
        mx = jnp.maximum(jnp.maximum(ms[0], ms[1]), ms[2])
        w = [jnp.exp(ms[g] - mx) for g in range(3)]
        den = w[0] * ls[0] + w[1] * ls[1] + w[2] * ls[2]
        o = (w[0] * accs[0] + w[1] * accs[1] + w[2] * accs[2]) / den
        o_ref[:, s * LANES:(s + 1) * LANES] = jnp.where(lo, o[0:1, :], o[1:2, :])


def _dilated_sample(us3, caches):
    n_dec = us3.shape[0]
    e = us3.shape[2]
    in_specs = [pl.BlockSpec((None, 1, e), lambda b: (b, 0, 0))]
    for c in caches:
        in_specs.append(pl.BlockSpec((None, c.shape[1], 2 * D_MODEL), lambda b: (b, 0, 0)))
    out = pl.pallas_call(
        _dilated_sample_body, grid=(n_dec,),
        in_specs=in_specs,
        out_specs=pl.BlockSpec((None, 1, D_MODEL), lambda b: (b, 0, 0)),
        out_shape=jax.ShapeDtypeStruct((n_dec, 1, D_MODEL), F32),
        compiler_params=_params(1), name="dilated_sample",
    )(us3, *caches)
    return out.reshape(n_dec, D_MODEL)


def _rope_tables(pos):
    half = HEAD_DIM // 2
    inv = ROPE_THETA ** (-jnp.arange(half, dtype=F32) / half)
    ang = pos.astype(F32)[:, None] * inv[None, :]
    cos, sin = jnp.cos(ang), jnp.sin(ang)
    cos = jnp.concatenate([cos, cos], axis=1)
    sin = jnp.concatenate([-sin, sin], axis=1)
    return jnp.tile(cos, (1, LANES // HEAD_DIM)), jnp.tile(sin, (1, LANES // HEAD_DIM))


def _head_gain(g):
    return jnp.tile(g.astype(F32), D_MODEL // HEAD_DIM)


def kernel(x_prompt, x_sample, cache_a_k, cache_a_v, cache_b_k, cache_b_v, cache_c_kv0, cache_c_kv1, cache_c_kv2,
           page_table, norm_a, w_in_a, qn_a, kn_a, w_out_a, norm_b, w_in_b, qn_b, kn_b, lam_q1_b, lam_k1_b,
           lam_q2_b, lam_k2_b, subln_b, w_out_b, norm_c, w_in_c, qn_c, kn_c, w_out_c):
    n, seq, _ = x_prompt.shape
    n_dec, dec_seq, _ = x_sample.shape
    assert dec_seq == 1 and seq % 512 == 0
    page = cache_a_k.shape[2]
    n_pages = page_table.shape[1]
    past = n_pages * page
    n_pool = cache_a_k.shape[1]
    depth = norm_a.shape[0] + norm_b.shape[0] + norm_c.shape[0]
    tm = 512
    a_heads = D_MODEL // HEAD_DIM
    b_heads = D_MODEL // (2 * HEAD_DIM)
    c_caches = (cache_c_kv0, cache_c_kv1, cache_c_kv2)

    cos_p, sin_p = _rope_tables(jnp.arange(seq, dtype=jnp.int32))
    cos_s, sin_s = _rope_tables(jnp.full((n_dec,), past, jnp.int32))
    bd = (jnp.arange(256)[:, None] // HEAD_DIM == jnp.arange(256)[None, :] // HEAD_DIM).astype(BF16)
    zero_gain = jnp.zeros((D_MODEL,), F32)

    hp = x_prompt.reshape(n * seq, D_MODEL)
    hs = x_sample.reshape(n_dec, D_MODEL)
    ak_p, av_p, ak_s, av_s = [], [], [], []
    bk_p, bv_p, bk_s, bv_s = [], [], [], []
    ckv_p = [[] for _ in C_PATTERNS]
    ckv_s = [[] for _ in C_PATTERNS]

    def project_both(norm, w_in, gains, modes):
        w_bf = w_in.astype(BF16)
        gains = gains.reshape(gains.shape[0], 1, COL_TILE)
        up = _project(hp, norm, w_bf, gains, modes, cos_p, sin_p, bd, tm)
        us = _project(hs, norm, w_bf, gains, modes, cos_s, sin_s, bd, n_dec)
        return up, us

    for layer in range(depth):
        kind, j = layer % 3, layer // 3
        if kind == 0:
            gains = jnp.stack([_head_gain(qn_a[j]), _head_gain(kn_a[j]), zero_gain, zero_gain])
            modes = jnp.array([1, 1, 0, 0], jnp.int32)
            up, us = project_both(norm_a[j], w_in_a[j], gains, modes)
            op = _moba_prompt(up.reshape(n, seq, 4 * D_MODEL)).reshape(n * seq, D_MODEL)
            us3 = us.reshape(n_dec, 4, a_heads, HEAD_DIM)
            qs_, kns_, vns_ = us3[:, 0], us3[:, 1], us3[:, 2]
            kmean = _moba_kmean(cache_a_k, j, page_table)
            sel = _moba_select(kmean, qs_).reshape(n_dec, MOBA_TOPK * a_heads)
            os_ = _moba_sample(cache_a_k, cache_a_v, j, page_table, sel, qs_, kns_, vns_)
            hp = _out_project(op, up, hp, w_out_a[j].astype(BF16), tm)
            hs = _out_project(os_, us, hs, w_out_a[j].astype(BF16), n_dec)
            ak_p.append(up[:, D_MODEL:2 * D_MODEL].reshape(n, seq, a_heads, HEAD_DIM))
            av_p.append(up[:, 2 * D_MODEL:3 * D_MODEL].reshape(n, seq, a_heads, HEAD_DIM))
            ak_s.append(us[:, D_MODEL:2 * D_MODEL].reshape(n_dec, 1, a_heads, HEAD_DIM))
            av_s.append(us[:, 2 * D_MODEL:3 * D_MODEL].reshape(n_dec, 1, a_heads, HEAD_DIM))
        elif kind == 1:
            lam_init = 0.8 - 0.6 * math.exp(-0.3 * layer)
            gains = jnp.stack([_head_gain(qn_b[j]), _head_gain(kn_b[j]), zero_gain, zero_gain])
            modes = jnp.array([1, 1, 0, 0], jnp.int32)
            up, us = project_both(norm_b[j], w_in_b[j], gains, modes)
            lams = tuple(v[j].astype(F32).reshape(1, HEAD_DIM) for v in (lam_q1_b, lam_k1_b, lam_q2_b, lam_k2_b))
            subln = subln_b[j].astype(F32).reshape(1, LANES)
            op = _diff_prompt(up.reshape(n, seq, 4 * D_MODEL), lams, subln, lam_init).reshape(n * seq, D_MODEL)
            pool_k = cache_b_k[j].reshape(n_pool, page * b_heads, LANES)
            pool_v = cache_b_v[j].reshape(n_pool, page * b_heads, LANES)
            os_ = _diff_sample(pool_k, pool_v, page_table, us.reshape(n_dec, 4, b_heads, LANES), lams, subln, lam_init)
            hp = _out_project(op, up, hp, w_out_b[j].astype(BF16), tm)
            hs = _out_project(os_, us, hs, w_out_b[j].astype(BF16), n_dec)
            bk_p.append(up[:, D_MODEL:2 * D_MODEL].reshape(n, seq, b_heads, LANES))
            bv_p.append(up[:, 2 * D_MODEL:3 * D_MODEL].reshape(n, seq, b_heads, LANES))
            bk_s.append(us[:, D_MODEL:2 * D_MODEL].reshape(n_dec, 1, b_heads, LANES))
            bv_s.append(us[:, 2 * D_MODEL:3 * D_MODEL].reshape(n_dec, 1, b_heads, LANES))
        else:
            gain_rows, mode_vals = [], []
            for g in range(len(C_PATTERNS)):
                gain_rows += [_head_gain(qn_c[j, g]), _head_gain(kn_c[j, g]), zero_gain]
                mode_vals += [1, 1, 0]
            gains = jnp.stack(gain_rows + [zero_gain])
            modes = jnp.array(mode_vals + [0], jnp.int32)
            up, us = project_both(norm_c[j], w_in_c[j], gains, modes)
            e = up.shape[1]
            op = _dilated_prompt(up.reshape(n, seq, e)).reshape(n * seq, D_MODEL)
            views = []
            for g, (win, dil) in enumerate(C_PATTERNS):
                buf = c_caches[g][j]
                n_buf = buf.shape[1]
                assert n_buf == win and win // dil == 128
                views.append(buf.reshape(n_dec, n_buf // dil, dil * 2 * D_MODEL))
            os_ = _dilated_sample(us.reshape(n_dec, 1, e), views)
            hp = _out_project(op, up, hp, w_out_c[j].astype(BF16), tm)
            hs = _out_project(os_, us, hs, w_out_c[j].astype(BF16), n_dec)
            up3 = up.reshape(n, seq, e)
            for g, (win, dil) in enumerate(C_PATTERNS):
                c0 = (3 * g + 1) * D_MODEL
                keep_p = min(win, seq)
                ckv_p[g].append(up3[:, seq - keep_p:, c0:c0 + 2 * D_MODEL].reshape(n, keep_p, 2, a_heads, HEAD_DIM))
                buf = c_caches[g][j]
                new = us[:, c0:c0 + 2 * D_MODEL].reshape(n_dec, 1, 2, a_heads, HEAD_DIM)
                ckv_s[g].append(jnp.concatenate([buf[:, 1:], new], axis=1))

    return (hp.reshape(n, seq, D_MODEL), hs.reshape(n_dec, 1, D_MODEL),
            jnp.stack(ak_p), jnp.stack(av_p), jnp.stack(ak_s), jnp.stack(av_s),
            jnp.stack(bk_p), jnp.stack(bv_p), jnp.stack(bk_s), jnp.stack(bv_s),
            jnp.stack(ckv_p[0]), jnp.stack(ckv_p[1]), jnp.stack(ckv_p[2]),
            jnp.stack(ckv_s[0]), jnp.stack(ckv_s[1]), jnp.stack(ckv_s[2]))
```

```python
import functools
import math

import jax
import jax.numpy as jnp
from jax import lax
from jax.experimental import pallas as pl
from jax.experimental.pallas import tpu as pltpu

F32 = jnp.float32
BF16 = jnp.bfloat16

D_MODEL = 1024
HEAD_DIM = 64
LANES = 128
COL_TILE = 1024
EPS = 1e-6
ROPE_THETA = 10000.0
SCALE = HEAD_DIM ** -0.5
NEG = -1e30
MOBA_BLOCK = 256
MOBA_TOPK = 3
C_PATTERNS = ((128, 1), (512, 4), (2048, 16))
VMEM_LIMIT = 48 * 1024 * 1024

_ABT = (((1,), (1,)), ((), ()))


def _params(n_axes):
    return pltpu.CompilerParams(dimension_semantics=("arbitrary",) * n_axes,
                                vmem_limit_bytes=VMEM_LIMIT)


def _lane_lo():
    return lax.broadcasted_iota(jnp.int32, (1, LANES), 1) < HEAD_DIM


def _row_picks_head():
    row = lax.broadcasted_iota(jnp.int32, (8, LANES), 0)
    lane = lax.broadcasted_iota(jnp.int32, (8, LANES), 1)
    return (lane // HEAD_DIM) == row


def _proj_body(mode_ref, x_ref, g_ref, w_ref, gain_ref, cos_ref, sin_ref, bd_ref, o_ref, xn_ref):
    j = pl.program_id(1)

    @pl.when(j == 0)
    def _():
        x = x_ref[...]
        ms = jnp.mean(x * x, axis=-1, keepdims=True)
        xn_ref[...] = (x * lax.rsqrt(ms + EPS) * g_ref[...]).astype(BF16)

    u = jnp.dot(xn_ref[...], w_ref[...], preferred_element_type=F32)
    mode = mode_ref[j]

    @pl.when(mode == 0)
    def _():
        o_ref[...] = u

    @pl.when(mode == 1)
    def _():
        gain = gain_ref[...]
        cos = cos_ref[...]
        sin = sin_ref[...]
        lane = lax.broadcasted_iota(jnp.int32, (1, LANES), 1)
        first_half = (lane % HEAD_DIM) < (HEAD_DIM // 2)
        for c in range(COL_TILE // 256):
            uc = u[:, c * 256:(c + 1) * 256]
            ss = jnp.dot((uc * uc).astype(BF16), bd_ref[...], preferred_element_type=F32)
            y = uc * lax.rsqrt(ss * (1.0 / HEAD_DIM) + EPS) * gain[:, c * 256:(c + 1) * 256]
            for s in range(2):
                ys = y[:, s * LANES:(s + 1) * LANES]
                rot = jnp.where(first_half, pltpu.roll(ys, 96, 1), pltpu.roll(ys, 32, 1))
                o_ref[:, c * 256 + s * LANES:c * 256 + (s + 1) * LANES] = ys * cos + rot * sin


def _project(x, g, w_bf, gains, modes, cos, sin, bd, tm):
    t, e = x.shape[0], w_bf.shape[1]
    nt = e // COL_TILE
    npb = cos.shape[0] // tm
    grid_spec = pltpu.PrefetchScalarGridSpec(
        num_scalar_prefetch=1,
        grid=(t // tm, nt),
        in_specs=[
            pl.BlockSpec((tm, D_MODEL), lambda i, j, m: (i, 0)),
            pl.BlockSpec((1, D_MODEL), lambda i, j, m: (0, 0)),
            pl.BlockSpec((D_MODEL, COL_TILE), lambda i, j, m: (0, j)),
            pl.BlockSpec((None, 1, COL_TILE), lambda i, j, m: (j, 0, 0)),
            pl.BlockSpec((tm, LANES), lambda i, j, m: (i % npb, 0)),
            pl.BlockSpec((tm, LANES), lambda i, j, m: (i % npb, 0)),
            pl.BlockSpec((256, 256), lambda i, j, m: (0, 0)),
        ],
        out_specs=pl.BlockSpec((tm, COL_TILE), lambda i, j, m: (i, j)),
        scratch_shapes=[pltpu.VMEM((tm, D_MODEL), BF16)],
    )
    return pl.pallas_call(
        _proj_body, grid_spec=grid_spec,
        out_shape=jax.ShapeDtypeStruct((t, e), F32),
        compiler_params=_params(2), name="proj_in",
    )(modes, x, g.reshape(1, D_MODEL), w_bf, gains, cos, sin, bd)


def _outproj_body(o_ref, gate_ref, h_ref, w_ref, y_ref):
    g = gate_ref[...]
    z = o_ref[...] * (g * jax.nn.sigmoid(g))
    y_ref[...] = h_ref[...] + jnp.dot(z.astype(BF16), w_ref[...], preferred_element_type=F32)


def _out_project(o, u, h, w_bf, tm):
    t = o.shape[0]
    gate_blk = u.shape[1] // COL_TILE - 1
    return pl.pallas_call(
        _outproj_body, grid=(t // tm,),
        in_specs=[
            pl.BlockSpec((tm, D_MODEL), lambda i: (i, 0)),
            pl.BlockSpec((tm, COL_TILE), lambda i: (i, gate_blk)),
            pl.BlockSpec((tm, D_MODEL), lambda i: (i, 0)),
            pl.BlockSpec((D_MODEL, D_MODEL), lambda i: (0, 0)),
        ],
        out_specs=pl.BlockSpec((tm, D_MODEL), lambda i: (i, 0)),
        out_shape=jax.ShapeDtypeStruct((t, D_MODEL), F32),
        compiler_params=_params(1), name="proj_out",
    )(o, u, h, w_bf)


def _two_pass_rows(qm, qi, rowpos, past_ok, kb_ref, vb_ref, s_ref, mx_ref, ls_ref, acc_ref, *, nkb, tk):
    mx_ref[...] = jnp.full(mx_ref.shape, NEG, F32)
    ls_ref[...] = jnp.zeros(ls_ref.shape, F32)
    acc_ref[...] = jnp.zeros(acc_ref.shape, F32)
    for kb in range(nkb):
        ks = slice(kb * tk, (kb + 1) * tk)

        @pl.when(kb < qi)
        def _():
            for h in range(2):
                s = lax.dot_general(qm[h], kb_ref[ks, :], _ABT, preferred_element_type=F32)
                ok = past_ok(h, kb)
                if ok is not None:
                    s = jnp.where(ok, s, NEG)
                s_ref[h, kb] = s
                mx_ref[h] = jnp.maximum(mx_ref[h], s)

        @pl.when(kb == qi)
        def _():
            colpos = kb * tk + lax.broadcasted_iota(jnp.int32, (1, tk), 1)
            ok = colpos <= rowpos
            for h in range(2):
                s = lax.dot_general(qm[h], kb_ref[ks, :], _ABT, preferred_element_type=F32)
                s = jnp.where(ok, s, NEG)
                s_ref[h, kb] = s
                mx_ref[h] = jnp.maximum(mx_ref[h], s)

    m = [jnp.max(mx_ref[h], axis=1, keepdims=True) for h in range(2)]
    for kb in range(nkb):
        ks = slice(kb * tk, (kb + 1) * tk)

        @pl.when(kb <= qi)
        def _():
            for h in range(2):
                p = jnp.exp(s_ref[h, kb] - m[h])
                ls_ref[h] = ls_ref[h] + p
                acc_ref[h] = acc_ref[h] + jnp.dot(p.astype(BF16), vb_ref[ks, :], preferred_element_type=F32)

    return [acc_ref[h] / jnp.sum(ls_ref[h], axis=1, keepdims=True) for h in range(2)]


def _two_pass_scratch(seq, tq, tk):
    return [
        pltpu.VMEM((seq, LANES), BF16), pltpu.VMEM((seq, LANES), BF16),
        pltpu.VMEM((2, seq // tk, tq, tk), F32),
        pltpu.VMEM((2, tq, tk), F32), pltpu.VMEM((2, tq, tk), F32), pltpu.VMEM((2, tq, LANES), F32),
    ]


def _moba_prompt_body(q_ref, k_ref, v_ref, o_ref, kb_ref, vb_ref, s_ref, mx_ref, ls_ref, acc_ref, *, seq):
    tq = MOBA_BLOCK
    nb = seq // MOBA_BLOCK
    lo = _lane_lo()
    kb_ref[...] = k_ref[...].astype(BF16)
    vb_ref[...] = v_ref[...].astype(BF16)
    kmean = jnp.concatenate(
        [jnp.mean(k_ref[b * MOBA_BLOCK:(b + 1) * MOBA_BLOCK, :], axis=0, keepdims=True) for b in range(nb)],
        axis=0)
    kcat = jnp.concatenate([jnp.where(lo, kmean, 0.0), jnp.where(lo, 0.0, kmean)], axis=0)

    def qblock(qi, carry):
        r0 = pl.multiple_of(qi * tq, tq)
        q = q_ref[pl.ds(r0, tq), :]
        gate = lax.dot_general(q, kcat, _ABT, precision=lax.Precision.HIGHEST,
                               preferred_element_type=F32)
        qs = q * SCALE
        qm = (jnp.where(lo, qs, 0.0).astype(BF16), jnp.where(lo, 0.0, qs).astype(BF16))
        bl = lax.broadcasted_iota(jnp.int32, (tq, nb), 1)
        ownv = jnp.full((tq, 1), qi, jnp.int32)
        sels = []
        for h in range(2):
            gh = gate[:, h * nb:(h + 1) * nb]
            cnt = jnp.zeros((tq, nb), F32)
            for b2 in range(nb):
                col = gh[:, b2:b2 + 1]
                beats = jnp.where(col > gh, 1.0, jnp.where(col == gh, jnp.where(bl > b2, 1.0, 0.0), 0.0))
                cnt = cnt + jnp.where(ownv > b2, beats, 0.0)
            sels.append(jnp.where(bl < ownv, jnp.where(cnt < MOBA_TOPK, 1.0, 0.0), 0.0))
        rowpos = r0 + lax.broadcasted_iota(jnp.int32, (tq, 1), 0)
        outs = _two_pass_rows(qm, qi, rowpos, lambda h, kb: sels[h][:, kb:kb + 1] > 0.5,
                              kb_ref, vb_ref, s_ref, mx_ref, ls_ref, acc_ref, nkb=nb, tk=MOBA_BLOCK)
        o_ref[pl.ds(r0, tq), :] = jnp.where(lo, outs[0], outs[1])
        return carry

    lax.fori_loop(0, seq // tq, qblock, 0)


def _moba_prompt(u3):
    n, seq, _ = u3.shape
    nslab = D_MODEL // LANES
    body = functools.partial(_moba_prompt_body, seq=seq)
    return pl.pallas_call(
        body, grid=(n, nslab),
        in_specs=[
            pl.BlockSpec((None, seq, LANES), lambda b, s: (b, 0, s)),
            pl.BlockSpec((None, seq, LANES), lambda b, s: (b, 0, nslab + s)),
            pl.BlockSpec((None, seq, LANES), lambda b, s: (b, 0, 2 * nslab + s)),
        ],
        out_specs=pl.BlockSpec((None, seq, LANES), lambda b, s: (b, 0, s)),
        out_shape=jax.ShapeDtypeStruct((n, seq, D_MODEL), F32),
        scratch_shapes=_two_pass_scratch(seq, MOBA_BLOCK, MOBA_BLOCK),
        compiler_params=_params(2), name="moba_prompt",
    )(u3, u3, u3)


def _lambda_value(lq1_ref, lk1_ref, lq2_ref, lk2_ref, lam_init):
    a = jnp.sum(lq1_ref[...] * lk1_ref[...], axis=1, keepdims=True)
    b = jnp.sum(lq2_ref[...] * lk2_ref[...], axis=1, keepdims=True)
    return jnp.exp(a) - jnp.exp(b) + lam_init


def _diff_finish(a1, a2, lam, subln, lam_init):
    o = a1 - lam * a2
    ms = jnp.mean(o * o, axis=-1, keepdims=True)
    return (o * lax.rsqrt(ms + EPS) * subln) * (1.0 - lam_init)


def _diff_prompt_body(q_ref, k_ref, v_ref, lq1_ref, lk1_ref, lq2_ref, lk2_ref, subln_ref, o_ref,
                      kb_ref, vb_ref, s_ref, mx_ref, ls_ref, acc_ref, *, seq, tq, lam_init):
    lo = _lane_lo()
    kb_ref[...] = k_ref[...].astype(BF16)
    vb_ref[...] = v_ref[...].astype(BF16)
    lam = _lambda_value(lq1_ref, lk1_ref, lq2_ref, lk2_ref, lam_init)
    subln = subln_ref[...]

    def qblock(qi, carry):
        r0 = pl.multiple_of(qi * tq, tq)
        qs = q_ref[pl.ds(r0, tq), :] * SCALE
        qm = (jnp.where(lo, qs, 0.0).astype(BF16), jnp.where(lo, 0.0, qs).astype(BF16))
        rowpos = r0 + lax.broadcasted_iota(jnp.int32, (tq, 1), 0)
        outs = _two_pass_rows(qm, qi, rowpos, lambda h, kb: None,
                              kb_ref, vb_ref, s_ref, mx_ref, ls_ref, acc_ref, nkb=seq // tq, tk=tq)
        o_ref[pl.ds(r0, tq), :] = _diff_finish(outs[0], outs[1], lam, subln, lam_init)
        return carry

    lax.fori_loop(0, seq // tq, qblock, 0)


def _diff_prompt(u3, lams, subln, lam_init, tq=256):
    n, seq, _ = u3.shape
    nslab = D_MODEL // LANES
    body = functools.partial(_diff_prompt_body, seq=seq, tq=tq, lam_init=lam_init)
    vec = pl.BlockSpec((1, HEAD_DIM), lambda b, s: (0, 0))
    return pl.pallas_call(
        body, grid=(n, nslab),
        in_specs=[
            pl.BlockSpec((None, seq, LANES), lambda b, s: (b, 0, s)),
            pl.BlockSpec((None, seq, LANES), lambda b, s: (b, 0, nslab + s)),
            pl.BlockSpec((None, seq, LANES), lambda b, s: (b, 0, 2 * nslab + s)),
            vec, vec, vec, vec,
            pl.BlockSpec((1, LANES), lambda b, s: (0, 0)),
        ],
        out_specs=pl.BlockSpec((None, seq, LANES), lambda b, s: (b, 0, s)),
        out_shape=jax.ShapeDtypeStruct((n, seq, D_MODEL), F32),
        scratch_shapes=_two_pass_scratch(seq, tq, tq),
        compiler_params=_params(2), name="diff_prompt",
    )(u3, u3, u3, *lams, subln)


def _dilated_prompt_body(*refs, seq, unroll):
    qkv_refs = refs[:9]
    o_ref = refs[9]
    qc_ref, kc_ref, vc_ref, ocm_ref, lcm_ref, og_ref, lg_ref = refs[10:]
    lo = _lane_lo()
    blk = 128
    nblk = seq // blk
    ri = lax.broadcasted_iota(jnp.int32, (blk, 2 * blk), 0)
    ci = lax.broadcasted_iota(jnp.int32, (blk, 2 * blk), 1)
    bias_band = jnp.where((ci >= ri) & (ci <= ri + blk), 0.0, NEG)
    bias_own = jnp.where((ci >= blk) & (ci <= ri + blk), 0.0, NEG)
    kc_ref[0:blk, :] = jnp.zeros((blk, LANES), BF16)
    vc_ref[0:blk, :] = jnp.zeros((blk, LANES), BF16)

    def attend(qb, kk, vv, bias):
        outs, lses = [], []
        for h in range(2):
            qm = jnp.where(lo, qb, 0.0) if h == 0 else jnp.where(lo, 0.0, qb)
            s = lax.dot_general(qm.astype(BF16), kk, _ABT, preferred_element_type=F32) + bias
            m = jnp.max(s, axis=1, keepdims=True)
            p = jnp.exp(s - m)
            l = jnp.sum(p, axis=1, keepdims=True)
            outs.append(jnp.dot(p.astype(BF16), vv, preferred_element_type=F32) / l)
            lses.append(m + jnp.log(l))
        return jnp.where(lo, outs[0], outs[1]), jnp.where(lo, lses[0], lses[1])

    for g, (win, dil) in enumerate(C_PATTERNS):
        q_ref, k_ref, v_ref = qkv_refs[3 * g:3 * g + 3]
        ncls = seq // dil
        bpc = ncls // blk
        for r in range(dil):
            rows = pl.ds(r, ncls, stride=dil) if dil > 1 else slice(None)
            qc_ref[r * ncls:(r + 1) * ncls, :] = q_ref[rows, :] * SCALE
            kc_ref[blk + r * ncls:blk + (r + 1) * ncls, :] = k_ref[rows, :].astype(BF16)
            vc_ref[blk + r * ncls:blk + (r + 1) * ncls, :] = v_ref[rows, :].astype(BF16)

        def blocks(it, carry):
            for u in range(unroll):
                gb = it * unroll + u
                r0 = pl.multiple_of(gb * blk, blk)
                bias = jnp.where((gb % bpc) != 0, bias_band, bias_own)
                o, lse = attend(qc_ref[pl.ds(r0, blk), :], kc_ref[pl.ds(r0, 2 * blk), :],
                                vc_ref[pl.ds(r0, 2 * blk), :], bias)
                ocm_ref[pl.ds(r0, blk), :] = o
                lcm_ref[pl.ds(r0, blk), :] = lse
            return carry

        lax.fori_loop(0, nblk // unroll, blocks, 0)
        for r in range(dil):
            rows = pl.ds(r, ncls, stride=dil) if dil > 1 else slice(None)
            src = slice(r * ncls, (r + 1) * ncls)
            og_ref[g, rows, :] = ocm_ref[src, :]
            lg_ref[g, rows, :] = lcm_ref[src, :]

    lmax = jnp.maximum(jnp.maximum(lg_ref[0], lg_ref[1]), lg_ref[2])
    w = [jnp.exp(lg_ref[g] - lmax) for g in range(3)]
    den = w[0] + w[1] + w[2]
    o_ref[...] = (w[0] * og_ref[0] + w[1] * og_ref[1] + w[2] * og_ref[2]) / den


def _dilated_prompt(u3):
    n, seq, _ = u3.shape
    nslab = D_MODEL // LANES
    in_specs = []
    for g in range(3):
        for c in range(3):
            base = (3 * g + c) * nslab
            in_specs.append(pl.BlockSpec((None, seq, LANES), lambda b, s, base=base: (b, 0, base + s)))
    body = functools.partial(_dilated_prompt_body, seq=seq, unroll=4)
    return pl.pallas_call(
        body, grid=(n, nslab),
        in_specs=in_specs,
        out_specs=pl.BlockSpec((None, seq, LANES), lambda b, s: (b, 0, s)),
        out_shape=jax.ShapeDtypeStruct((n, seq, D_MODEL), F32),
        scratch_shapes=[
            pltpu.VMEM((seq, LANES), F32), pltpu.VMEM((seq + 128, LANES), BF16), pltpu.VMEM((seq + 128, LANES), BF16),
            pltpu.VMEM((seq, LANES), F32), pltpu.VMEM((seq, LANES), F32),
            pltpu.VMEM((3, seq, LANES), F32), pltpu.VMEM((3, seq, LANES), F32),
        ],
        compiler_params=_params(2), name="dilated_prompt",
    )(*([u3] * 9))


def _kmean_body(pt_ref, *refs, pages_per_block, pps):
    k_refs, o_ref = refs[:pps], refs[pps]
    p = pl.program_id(1)
    heads, dim, page = k_refs[0].shape
    ones = jnp.ones((8, page), F32)
    bps = pps // pages_per_block
    for b in range(bps):
        s = k_refs[b * pages_per_block][...]
        for i in range(1, pages_per_block):
            s = s + k_refs[b * pages_per_block + i][...]
        r = lax.dot_general(ones, s.reshape(heads * dim, page), _ABT, precision=lax.Precision.HIGHEST,
                            preferred_element_type=F32)
        o_ref[pl.ds(p * bps + b, 1), :] = r[0:1, :] * (1.0 / (pages_per_block * page))


def _moba_kmean(cache_t, layer, page_table, pps=8):
    n_dec, n_pages = page_table.shape
    _, _, heads, dim, page = cache_t.shape
    ppb = MOBA_BLOCK // page
    assert pps % ppb == 0 and n_pages % pps == 0
    pages = [pl.BlockSpec((None, None, heads, dim, page), lambda b, p, pt, i=i: (layer, pt[b, pps * p + i], 0, 0, 0))
             for i in range(pps)]
    grid_spec = pltpu.PrefetchScalarGridSpec(
        num_scalar_prefetch=1, grid=(n_dec, n_pages // pps),
        in_specs=pages,
        out_specs=pl.BlockSpec((None, n_pages // ppb, heads * dim), lambda b, p, pt: (b, 0, 0)),
    )
    return pl.pallas_call(
        functools.partial(_kmean_body, pages_per_block=ppb, pps=pps), grid_spec=grid_spec,
        out_shape=jax.ShapeDtypeStruct((n_dec, n_pages // ppb, heads * dim), F32),
        compiler_params=_params(2), name="moba_kmean",
    )(page_table, *([cache_t] * pps))


def _moba_select_body(km_ref, q_ref, seg_ref, o_ref):
    nbk = km_ref.shape[0]
    gate = jnp.dot(km_ref[...] * q_ref[...], seg_ref[...], precision=lax.Precision.HIGHEST,
                   preferred_element_type=F32)
    rows = lax.broadcasted_iota(jnp.int32, (nbk, LANES), 0)
    o_ref[...] = jnp.zeros(o_ref.shape, jnp.int32)
    for j in range(MOBA_TOPK):
        mx = jnp.max(gate, axis=0, keepdims=True)
        idx = jnp.min(jnp.where(gate == mx, rows, nbk), axis=0, keepdims=True)
        o_ref[j:j + 1, :] = idx
        gate = jnp.where(rows == idx, -jnp.inf, gate)


def _moba_select(kmean, q3, seg):
    n_dec, nbk, _ = kmean.shape
    return pl.pallas_call(
        _moba_select_body, grid=(n_dec,),
        in_specs=[
            pl.BlockSpec((None, nbk, D_MODEL), lambda b: (b, 0, 0)),
            pl.BlockSpec((None, 1, D_MODEL), lambda b: (b, 0, 0)),
            pl.BlockSpec((D_MODEL, LANES), lambda b: (0, 0)),
        ],
        out_specs=pl.BlockSpec((None, 8, LANES), lambda b: (b, 0, 0)),
        out_shape=jax.ShapeDtypeStruct((n_dec, 8, LANES), jnp.int32),
        compiler_params=_params(1), name="moba_select",
    )(kmean, q3, seg)


def _moba_sample_body(pt_ref, sel_ref, q_ref, kn_ref, vn_ref, kpool, vpool, o_ref, kbuf, vbuf, sem,
                      *, layer, n_heads, ppb):
    b = pl.program_id(0)
    n_chunks = MOBA_TOPK * ppb
    slot = b % 2

    def copies(bb, sl):
        out = []
        for h in range(n_heads):
            for c in range(n_chunks):
                pg = pt_ref[bb, ppb * sel_ref[bb, (c // ppb) * n_heads + h] + c % ppb]
                out.append(pltpu.make_async_copy(kpool.at[layer, pg, h], kbuf.at[sl, h, c], sem.at[0, sl]))
                out.append(pltpu.make_async_copy(vpool.at[layer, pg, h], vbuf.at[sl, h, c], sem.at[1, sl]))
        return out

    @pl.when(b == 0)
    def _():
        for cp in copies(b, slot):
            cp.start()

    @pl.when(b + 1 < pl.num_programs(0))
    def _():
        for cp in copies(b + 1, 1 - slot):
            cp.start()

    for cp in copies(b, slot):
        cp.wait()

    q = q_ref[...] * SCALE
    s_self = jnp.sum(q * kn_ref[...], axis=-1, keepdims=True)
    for h in range(n_heads):
        qh = jnp.broadcast_to(q[h:h + 1, :], (8, q.shape[1])).astype(BF16)
        s = jnp.concatenate([jnp.dot(qh, kbuf[slot, h, c].astype(BF16), preferred_element_type=F32)
                             for c in range(n_chunks)], axis=1)
        ss = s_self[h:h + 1, :]
        m = jnp.maximum(jnp.max(s, axis=1, keepdims=True), ss)
        p = jnp.exp(s - m)
        ps = jnp.exp(ss - m)
        den = jnp.sum(p, axis=1, keepdims=True) + ps
        pb = p.astype(BF16)
        page = kbuf.shape[-1]
        o = ps * vn_ref[h:h + 1, :]
        for c in range(n_chunks):
            o = o + lax.dot_general(pb[:, c * page:(c + 1) * page], vbuf[slot, h, c].astype(BF16), _ABT,
                                    preferred_element_type=F32)
        o_ref[h:h + 1, :] = (o / den)[0:1, :]


def _moba_sample(cache_kt, cache_vt, layer, page_table, sel, q, kn, vn):
    n_dec = page_table.shape[0]
    _, _, heads, dim, page = cache_kt.shape
    ppb = MOBA_BLOCK // page
    n_chunks = MOBA_TOPK * ppb
    row = pl.BlockSpec((None, heads, dim), lambda b, pt, idx: (b, 0, 0))
    grid_spec = pltpu.PrefetchScalarGridSpec(
        num_scalar_prefetch=2, grid=(n_dec,),
        in_specs=[row, row, row, pl.BlockSpec(memory_space=pl.ANY), pl.BlockSpec(memory_space=pl.ANY)],
        out_specs=row,
        scratch_shapes=[pltpu.VMEM((2, heads, n_chunks, dim, page), F32),
                        pltpu.VMEM((2, heads, n_chunks, dim, page), F32),
                        pltpu.SemaphoreType.DMA((2, 2))],
    )
    out = pl.pallas_call(
        functools.partial(_moba_sample_body, layer=layer, n_heads=heads, ppb=ppb), grid_spec=grid_spec,
        out_shape=jax.ShapeDtypeStruct((n_dec, heads, dim), F32),
        compiler_params=_params(1), name="moba_sample",
    )(page_table, sel, q, kn, vn, cache_kt, cache_vt)
    return out.reshape(n_dec, heads * dim)


def _diff_sample_body(pt_ref, q_ref, kn_ref, vn_ref, lq1_ref, lk1_ref, lq2_ref, lk2_ref, subln_ref, *refs,
                      n_heads, page, pps, lam_init):
    k_refs, v_refs = refs[:pps], refs[pps:2 * pps]
    o_ref, m_ref, l_ref, acc_ref = refs[2 * pps:]
    pg = pl.program_id(1)
    nrow = 2 * n_heads
    row = lax.broadcasted_iota(jnp.int32, (nrow, LANES), 0)
    lane = lax.broadcasted_iota(jnp.int32, (nrow, LANES), 1)
    rowcol = lax.broadcasted_iota(jnp.int32, (nrow, 1), 0)

    @pl.when(pg == 0)
    def _():
        m_ref[...] = jnp.full(m_ref.shape, NEG, F32)
        l_ref[...] = jnp.zeros(l_ref.shape, F32)
        acc_ref[...] = jnp.zeros(acc_ref.shape, F32)

    lhs = [jnp.where(row == 2 * h + lane // HEAD_DIM, q_ref[h:h + 1, :] * SCALE, 0.0) for h in range(n_heads)]
    lhs_bf = [x.astype(BF16) for x in lhs]
    parts = []
    for i in range(pps):
        s = None
        for h in range(n_heads):
            kh = k_refs[i][pl.ds(h, page, stride=n_heads), :].astype(BF16)
            d = lax.dot_general(lhs_bf[h], kh, _ABT, preferred_element_type=F32)
            s = d if s is None else s + d
        parts.append(s)
    s = jnp.concatenate(parts, axis=1)
    m_old = m_ref[...]
    m_new = jnp.maximum(m_old, jnp.max(s, axis=1, keepdims=True))
    alpha = jnp.exp(m_old - m_new)
    p = jnp.exp(s - m_new)
    l_ref[...] = alpha * l_ref[...] + jnp.sum(p, axis=1, keepdims=True)
    m_ref[...] = m_new
    acc = alpha * acc_ref[...]
    for i in range(pps):
        pi = p[:, i * page:(i + 1) * page]
        for h in range(n_heads):
            ph = jnp.where(rowcol // 2 == h, pi, 0.0).astype(BF16)
            vh = v_refs[i][pl.ds(h, page, stride=n_heads), :].astype(BF16)
            acc = acc + jnp.dot(ph, vh, preferred_element_type=F32)
    acc_ref[...] = acc

    @pl.when(pg == pl.num_programs(1) - 1)
    def _():
        lam = _lambda_value(lq1_ref, lk1_ref, lq2_ref, lk2_ref, lam_init)
        q_all, kn_all, vn_all = lhs[0], None, None
        for h in range(1, n_heads):
            q_all = q_all + lhs[h]
        for h in range(n_heads):
            sel = rowcol // 2 == h
            kn_h = jnp.where(sel, kn_ref[h:h + 1, :], 0.0)
            vn_h = jnp.where(sel, vn_ref[h:h + 1, :], 0.0)
            kn_all = kn_h if kn_all is None else kn_all + kn_h
            vn_all = vn_h if vn_all is None else vn_all + vn_h
        s_self = jnp.sum(q_all * kn_all, axis=1, keepdims=True)
        m1 = m_ref[...]
        m2 = jnp.maximum(m1, s_self)
        a1 = jnp.exp(m1 - m2)
        p_self = jnp.exp(s_self - m2)
        a = (a1 * acc_ref[...] + p_self * vn_all) / (a1 * l_ref[...] + p_self)
        for h in range(n_heads):
            o_ref[h:h + 1, :] = _diff_finish(a[2 * h:2 * h + 1, :], a[2 * h + 1:2 * h + 2, :], lam,
                                             subln_ref[...], lam_init)


def _diff_sample(pool_k, pool_v, page_table, us4, lams, subln, lam_init, pps=4):
    n_dec, n_pages = page_table.shape
    n_heads = us4.shape[2]
    rows = pool_k.shape[1]
    assert n_pages % pps == 0
    vec = pl.BlockSpec((1, HEAD_DIM), lambda b, p, pt: (0, 0))
    pages = [pl.BlockSpec((None, rows, LANES), lambda b, p, pt, i=i: (pt[b, pps * p + i], 0, 0)) for i in range(pps)]
    grid_spec = pltpu.PrefetchScalarGridSpec(
        num_scalar_prefetch=1, grid=(n_dec, n_pages // pps),
        in_specs=[
            pl.BlockSpec((None, None, n_heads, LANES), lambda b, p, pt: (b, 0, 0, 0)),
            pl.BlockSpec((None, None, n_heads, LANES), lambda b, p, pt: (b, 1, 0, 0)),
            pl.BlockSpec((None, None, n_heads, LANES), lambda b, p, pt: (b, 2, 0, 0)),
            vec, vec, vec, vec,
            pl.BlockSpec((1, LANES), lambda b, p, pt: (0, 0)),
        ] + pages + pages,
        out_specs=pl.BlockSpec((None, n_heads, LANES), lambda b, p, pt: (b, 0, 0)),
        scratch_shapes=[pltpu.VMEM((2 * n_heads, 1), F32), pltpu.VMEM((2 * n_heads, 1), F32),
                        pltpu.VMEM((2 * n_heads, LANES), F32)],
    )
    body = functools.partial(_diff_sample_body, n_heads=n_heads, page=rows // n_heads, pps=pps, lam_init=lam_init)
    out = pl.pallas_call(
        body, grid_spec=grid_spec,
        out_shape=jax.ShapeDtypeStruct((n_dec, n_heads, LANES), F32),
        compiler_params=_params(2), name="diff_sample",
    )(page_table, us4, us4, us4, *lams, subln, *([pool_k] * pps), *([pool_v] * pps))
    return out.reshape(n_dec, D_MODEL)


def _dilated_sample_body(qkv_ref, new_ref, c0_ref, c1_ref, c2_ref, o_ref, n0_ref, n1_ref, n2_ref, *, hps):
    b = pl.program_id(0)
    c_refs = (c0_ref, c1_ref, c2_ref)
    n_refs = (n0_ref, n1_ref, n2_ref)
    n_lane = lax.broadcasted_iota(jnp.int32, (1, new_ref.shape[-1]), 1)
    for h in range(hps):
        ms, ls, accs = [], [], []
        for g, (win, dil) in enumerate(C_PATTERNS):
            rows = c_refs[g].shape[-1]
            q = jnp.broadcast_to(qkv_ref[3 * g, h:h + 1, :] * SCALE, (8, HEAD_DIM))
            kn = qkv_ref[3 * g + 1, h:h + 1, :]
            vn = qkv_ref[3 * g + 2, h:h + 1, :]
            kt = c_refs[g][0, h]
            vt = c_refs[g][1, h]
            lane = lax.broadcasted_iota(jnp.int32, (1, rows), 1)
            sc = jnp.dot(q.astype(BF16), kt.astype(BF16), preferred_element_type=F32)
            if dil > 1:
                sc = jnp.where(lane % dil == 0, sc, NEG)
            s_self = jnp.sum(q * kn, axis=1, keepdims=True)
            m = jnp.maximum(jnp.max(sc, axis=1, keepdims=True), s_self)
            p = jnp.exp(sc - m)
            p_self = jnp.exp(s_self - m)
            ms.append(m)
            ls.append(jnp.sum(p, axis=1, keepdims=True) + p_self)
            accs.append(lax.dot_general(p.astype(BF16), vt.astype(BF16), _ABT, preferred_element_type=F32)
                        + p_self * vn)
            for kv, x in ((0, kt), (1, vt)):
                col = jnp.sum(jnp.where(n_lane == b, new_ref[g, kv, h], 0.0), axis=1, keepdims=True)
                n_refs[g][kv, h] = jnp.where(lane == rows - 1, col, pltpu.roll(x, rows - 1, 1))
        mx = jnp.maximum(jnp.maximum(ms[0], ms[1]), ms[2])
        w = [jnp.exp(ms[g] - mx) for g in range(3)]
        den = w[0] * ls[0] + w[1] * ls[1] + w[2] * ls[2]
        o = (w[0] * accs[0] + w[1] * accs[1] + w[2] * accs[2]) / den
        o_ref[h:h + 1, :] = o[0:1, :]


def _dilated_sample(qkv, new_t, caches_t, hps=4):
    n_dec = qkv.shape[0]
    heads = caches_t[0].shape[2]
    hb = heads // hps
    in_specs = [
        pl.BlockSpec((None, 9, None, hps, HEAD_DIM), lambda b, s: (b, 0, s, 0, 0)),
        pl.BlockSpec((3, 2, hps, HEAD_DIM, n_dec), lambda b, s: (0, 0, s, 0, 0)),
    ]
    cache_specs = [pl.BlockSpec((None, 2, hps, HEAD_DIM, c.shape[-1]), lambda b, s: (b, 0, s, 0, 0)) for c in caches_t]
    outs = pl.pallas_call(
        functools.partial(_dilated_sample_body, hps=hps), grid=(n_dec, hb),
        in_specs=in_specs + cache_specs,
        out_specs=[pl.BlockSpec((None, None, hps, HEAD_DIM), lambda b, s: (b, s, 0, 0))] + cache_specs,
        out_shape=[jax.ShapeDtypeStruct((n_dec, hb, hps, HEAD_DIM), F32)]
        + [jax.ShapeDtypeStruct(c.shape, F32) for c in caches_t],
        compiler_params=_params(2), name="dilated_sample",
    )(qkv, new_t, *caches_t)
    return outs[0].reshape(n_dec, heads * HEAD_DIM), outs[1:]


def _rope_tables(pos):
    half = HEAD_DIM // 2
    inv = ROPE_THETA ** (-jnp.arange(half, dtype=F32) / half)
    ang = pos.astype(F32)[:, None] * inv[None, :]
    cos, sin = jnp.cos(ang), jnp.sin(ang)
    cos = jnp.concatenate([cos, cos], axis=1)
    sin = jnp.concatenate([-sin, sin], axis=1)
    return jnp.tile(cos, (1, LANES // HEAD_DIM)), jnp.tile(sin, (1, LANES // HEAD_DIM))


def _head_gain(g):
    return jnp.tile(g.astype(F32), D_MODEL // HEAD_DIM)


def kernel(x_prompt, x_sample, cache_a_k, cache_a_v, cache_b_k, cache_b_v, cache_c_kv0, cache_c_kv1, cache_c_kv2,
           page_table, norm_a, w_in_a, qn_a, kn_a, w_out_a, norm_b, w_in_b, qn_b, kn_b, lam_q1_b, lam_k1_b,
           lam_q2_b, lam_k2_b, subln_b, w_out_b, norm_c, w_in_c, qn_c, kn_c, w_out_c):
    n, seq, _ = x_prompt.shape
    n_dec, dec_seq, _ = x_sample.shape
    assert dec_seq == 1 and seq % 512 == 0
    page = cache_a_k.shape[2]
    n_pages = page_table.shape[1]
    past = n_pages * page
    n_pool = cache_a_k.shape[1]
    depth = norm_a.shape[0] + norm_b.shape[0] + norm_c.shape[0]
    tm = 512
    a_heads = D_MODEL // HEAD_DIM
    b_heads = D_MODEL // (2 * HEAD_DIM)
    c_caches = (cache_c_kv0, cache_c_kv1, cache_c_kv2)

    cos_p, sin_p = _rope_tables(jnp.arange(seq, dtype=jnp.int32))
    cos_s, sin_s = _rope_tables(jnp.full((n_dec,), past, jnp.int32))
    bd = (jnp.arange(256)[:, None] // HEAD_DIM == jnp.arange(256)[None, :] // HEAD_DIM).astype(BF16)
    zero_gain = jnp.zeros((D_MODEL,), F32)
    seg = (jnp.arange(D_MODEL)[:, None] // HEAD_DIM == jnp.arange(LANES)[None, :]).astype(F32)
    cache_a_kt = jnp.transpose(cache_a_k, (0, 1, 3, 4, 2))
    cache_a_vt = jnp.transpose(cache_a_v, (0, 1, 3, 4, 2))

    hp = x_prompt.reshape(n * seq, D_MODEL)
    hs = x_sample.reshape(n_dec, D_MODEL)
    ak_p, av_p, ak_s, av_s = [], [], [], []
    bk_p, bv_p, bk_s, bv_s = [], [], [], []
    ckv_p = [[] for _ in C_PATTERNS]
    ckv_s = [[] for _ in C_PATTERNS]

    def project_both(norm, w_in, gains, modes):
        w_bf = w_in.astype(BF16)
        gains = gains.reshape(gains.shape[0], 1, COL_TILE)
        up = _project(hp, norm, w_bf, gains, modes, cos_p, sin_p, bd, tm)
        us = _project(hs, norm, w_bf, gains, modes, cos_s, sin_s, bd, n_dec)
        return up, us

    for layer in range(depth):
        kind, j = layer % 3, layer // 3
        if kind == 0:
            gains = jnp.stack([_head_gain(qn_a[j]), _head_gain(kn_a[j]), zero_gain, zero_gain])
            modes = jnp.array([1, 1, 0, 0], jnp.int32)
            up, us = project_both(norm_a[j], w_in_a[j], gains, modes)
            op = _moba_prompt(up.reshape(n, seq, 4 * D_MODEL)).reshape(n * seq, D_MODEL)
            us3 = us.reshape(n_dec, 4, a_heads, HEAD_DIM)
            kmean = _moba_kmean(cache_a_kt, j, page_table)
            sel = _moba_select(kmean, us.reshape(n_dec, 1, 4 * D_MODEL), seg)
            sel = sel[:, :MOBA_TOPK, :a_heads].reshape(n_dec, MOBA_TOPK * a_heads)
            os_ = _moba_sample(cache_a_kt, cache_a_vt, j, page_table, sel, us3[:, 0], us3[:, 1], us3[:, 2])
            hp = _out_project(op, up, hp, w_out_a[j].astype(BF16), tm)
            hs = _out_project(os_, us, hs, w_out_a[j].astype(BF16), n_dec)
            ak_p.append(up[:, D_MODEL:2 * D_MODEL].reshape(n, seq, a_heads, HEAD_DIM))
            av_p.append(up[:, 2 * D_MODEL:3 * D_MODEL].reshape(n, seq, a_heads, HEAD_DIM))
            ak_s.append(us[:, D_MODEL:2 * D_MODEL].reshape(n_dec, 1, a_heads, HEAD_DIM))
            av_s.append(us[:, 2 * D_MODEL:3 * D_MODEL].reshape(n_dec, 1, a_heads, HEAD_DIM))
        elif kind == 1:
            lam_init = 0.8 - 0.6 * math.exp(-0.3 * layer)
            gains = jnp.stack([_head_gain(qn_b[j]), _head_gain(kn_b[j]), zero_gain, zero_gain])
            modes = jnp.array([1, 1, 0, 0], jnp.int32)
            up, us = project_both(norm_b[j], w_in_b[j], gains, modes)
            lams = tuple(v[j].astype(F32).reshape(1, HEAD_DIM) for v in (lam_q1_b, lam_k1_b, lam_q2_b, lam_k2_b))
            subln = subln_b[j].astype(F32).reshape(1, LANES)
            op = _diff_prompt(up.reshape(n, seq, 4 * D_MODEL), lams, subln, lam_init).reshape(n * seq, D_MODEL)
            pool_k = cache_b_k[j].reshape(n_pool, page * b_heads, LANES)
            pool_v = cache_b_v[j].reshape(n_pool, page * b_heads, LANES)
            os_ = _diff_sample(pool_k, pool_v, page_table, us.reshape(n_dec, 4, b_heads, LANES), lams, subln, lam_init)
            hp = _out_project(op, up, hp, w_out_b[j].astype(BF16), tm)
            hs = _out_project(os_, us, hs, w_out_b[j].astype(BF16), n_dec)
            bk_p.append(up[:, D_MODEL:2 * D_MODEL].reshape(n, seq, b_heads, LANES))
            bv_p.append(up[:, 2 * D_MODEL:3 * D_MODEL].reshape(n, seq, b_heads, LANES))
            bk_s.append(us[:, D_MODEL:2 * D_MODEL].reshape(n_dec, 1, b_heads, LANES))
            bv_s.append(us[:, 2 * D_MODEL:3 * D_MODEL].reshape(n_dec, 1, b_heads, LANES))
        else:
            gain_rows, mode_vals = [], []
            for g in range(len(C_PATTERNS)):
                gain_rows += [_head_gain(qn_c[j, g]), _head_gain(kn_c[j, g]), zero_gain]
                mode_vals += [1, 1, 0]
            gains = jnp.stack(gain_rows + [zero_gain])
            modes = jnp.array(mode_vals + [0], jnp.int32)
            up, us = project_both(norm_c[j], w_in_c[j], gains, modes)
            e = up.shape[1]
            op = _dilated_prompt(up.reshape(n, seq, e)).reshape(n * seq, D_MODEL)
            n_grp = len(C_PATTERNS)
            hps = 4
            qkv = us[:, :3 * n_grp * D_MODEL].reshape(n_dec, 3 * n_grp, a_heads // hps, hps, HEAD_DIM)
            new_kv = us[:, :3 * n_grp * D_MODEL].reshape(n_dec, n_grp, 3, a_heads, HEAD_DIM)[:, :, 1:]
            new_t = jnp.transpose(new_kv, (1, 2, 3, 4, 0))
            caches_t = []
            for g, (win, dil) in enumerate(C_PATTERNS):
                assert c_caches[g].shape[2] == win and win // dil == 128
                caches_t.append(jnp.transpose(c_caches[g][j], (0, 2, 3, 4, 1)))
            os_, adv = _dilated_sample(qkv, new_t, caches_t, hps)
            hp = _out_project(op, up, hp, w_out_c[j].astype(BF16), tm)
            hs = _out_project(os_, us, hs, w_out_c[j].astype(BF16), n_dec)
            up3 = up.reshape(n, seq, e)
            for g, (win, dil) in enumerate(C_PATTERNS):
                c0 = (3 * g + 1) * D_MODEL
                keep_p = min(win, seq)
                ckv_p[g].append(up3[:, seq - keep_p:, c0:c0 + 2 * D_MODEL].reshape(n, keep_p, 2, a_heads, HEAD_DIM))
                ckv_s[g].append(jnp.transpose(adv[g], (0, 4, 1, 2, 3)))

    return (hp.reshape(n, seq, D_MODEL), hs.reshape(n_dec, 1, D_MODEL),
            jnp.stack(ak_p), jnp.stack(av_p), jnp.stack(ak_s), jnp.stack(av_s),
            jnp.stack(bk_p), jnp.stack(bv_p), jnp.stack(bk_s), jnp.stack(bv_s),
            jnp.stack(ckv_p[0]), jnp.stack(ckv_p[1]), jnp.stack(ckv_p[2]),
            jnp.stack(ckv_s[0]), jnp.stack(ckv_s[1]), jnp.stack(ckv_s[2]))
```

```python
import functools
import math

import jax
import jax.numpy as jnp
from jax import lax
from jax.experimental import pallas as pl
from jax.experimental.pallas import tpu as pltpu

F32 = jnp.float32
BF16 = jnp.bfloat16

D_MODEL = 1024
HEAD_DIM = 64
LANES = 128
COL_TILE = 1024
EPS = 1e-6
ROPE_THETA = 10000.0
SCALE = HEAD_DIM ** -0.5
SCALE_LOG2E = SCALE * math.log2(math.e)
NEG = -1e30
MOBA_BLOCK = 256
MOBA_TOPK = 3
C_PATTERNS = ((128, 1), (512, 4), (2048, 16))
VMEM_LIMIT = 48 * 1024 * 1024

_ABT = (((1,), (1,)), ((), ()))


def _params(n_axes):
    return pltpu.CompilerParams(dimension_semantics=("arbitrary",) * n_axes,
                                vmem_limit_bytes=VMEM_LIMIT)


def _lane_lo():
    return lax.broadcasted_iota(jnp.int32, (1, LANES), 1) < HEAD_DIM


def _row_picks_head():
    row = lax.broadcasted_iota(jnp.int32, (8, LANES), 0)
    lane = lax.broadcasted_iota(jnp.int32, (8, LANES), 1)
    return (lane // HEAD_DIM) == row


def _proj_body(mode_ref, x_ref, g_ref, w_ref, gain_ref, cos_ref, sin_ref, bd_ref, o_ref, xn_ref):
    j = pl.program_id(1)

    @pl.when(j == 0)
    def _():
        x = x_ref[...]
        ms = jnp.mean(x * x, axis=-1, keepdims=True)
        xn_ref[...] = (x * lax.rsqrt(ms + EPS) * g_ref[...]).astype(BF16)

    u = jnp.dot(xn_ref[...], w_ref[...], preferred_element_type=F32)
    mode = mode_ref[j]

    @pl.when(mode == 0)
    def _():
        o_ref[...] = u

    @pl.when(mode == 1)
    def _():
        gain = gain_ref[...]
        cos = cos_ref[...]
        sin = sin_ref[...]
        lane = lax.broadcasted_iota(jnp.int32, (1, LANES), 1)
        first_half = (lane % HEAD_DIM) < (HEAD_DIM // 2)
        for c in range(COL_TILE // 256):
            uc = u[:, c * 256:(c + 1) * 256]
            ss = jnp.dot((uc * uc).astype(BF16), bd_ref[...], preferred_element_type=F32)
            y = uc * lax.rsqrt(ss * (1.0 / HEAD_DIM) + EPS) * gain[:, c * 256:(c + 1) * 256]
            for s in range(2):
                ys = y[:, s * LANES:(s + 1) * LANES]
                rot = jnp.where(first_half, pltpu.roll(ys, 96, 1), pltpu.roll(ys, 32, 1))
                o_ref[:, c * 256 + s * LANES:c * 256 + (s + 1) * LANES] = ys * cos + rot * sin


def _project(x, g, w_bf, gains, modes, cos, sin, bd, tm):
    t, e = x.shape[0], w_bf.shape[1]
    nt = e // COL_TILE
    npb = cos.shape[0] // tm
    grid_spec = pltpu.PrefetchScalarGridSpec(
        num_scalar_prefetch=1,
        grid=(t // tm, nt),
        in_specs=[
            pl.BlockSpec((tm, D_MODEL), lambda i, j, m: (i, 0)),
            pl.BlockSpec((1, D_MODEL), lambda i, j, m: (0, 0)),
            pl.BlockSpec((D_MODEL, COL_TILE), lambda i, j, m: (0, j)),
            pl.BlockSpec((None, 1, COL_TILE), lambda i, j, m: (j, 0, 0)),
            pl.BlockSpec((tm, LANES), lambda i, j, m: (i % npb, 0)),
            pl.BlockSpec((tm, LANES), lambda i, j, m: (i % npb, 0)),
            pl.BlockSpec((256, 256), lambda i, j, m: (0, 0)),
        ],
        out_specs=pl.BlockSpec((tm, COL_TILE), lambda i, j, m: (i, j)),
        scratch_shapes=[pltpu.VMEM((tm, D_MODEL), BF16)],
    )
    return pl.pallas_call(
        _proj_body, grid_spec=grid_spec,
        out_shape=jax.ShapeDtypeStruct((t, e), F32),
        compiler_params=_params(2), name="proj_in",
    )(modes, x, g.reshape(1, D_MODEL), w_bf, gains, cos, sin, bd)


def _outproj_body(o_ref, gate_ref, h_ref, w_ref, y_ref):
    g = gate_ref[...]
    z = o_ref[...] * (g * jax.nn.sigmoid(g))
    y_ref[...] = h_ref[...] + jnp.dot(z.astype(BF16), w_ref[...], preferred_element_type=F32)


def _out_project(o, u, h, w_bf, tm):
    t = o.shape[0]
    gate_blk = u.shape[1] // COL_TILE - 1
    return pl.pallas_call(
        _outproj_body, grid=(t // tm,),
        in_specs=[
            pl.BlockSpec((tm, D_MODEL), lambda i: (i, 0)),
            pl.BlockSpec((tm, COL_TILE), lambda i: (i, gate_blk)),
            pl.BlockSpec((tm, D_MODEL), lambda i: (i, 0)),
            pl.BlockSpec((D_MODEL, D_MODEL), lambda i: (0, 0)),
        ],
        out_specs=pl.BlockSpec((tm, D_MODEL), lambda i: (i, 0)),
        out_shape=jax.ShapeDtypeStruct((t, D_MODEL), F32),
        compiler_params=_params(1), name="proj_out",
    )(o, u, h, w_bf)


def _two_pass_rows(qm, qi, rowpos, kb_ref, vb_ref, s_ref, m_ref, l_ref, acc_ref, *, nkb, tk, group=4):
    half = tk // 2
    m_ref[...] = jnp.full(m_ref.shape, NEG, F32)
    l_ref[...] = jnp.zeros(l_ref.shape, F32)
    acc_ref[...] = jnp.zeros(acc_ref.shape, F32)

    def score_tiles(tiles):
        for h in range(2):
            tmax = None
            for kb, own in tiles:
                k_ref = kb_ref[h] if isinstance(kb_ref, (tuple, list)) else kb_ref
                s = lax.dot_general(qm[h], k_ref[kb * tk:(kb + 1) * tk, :], _ABT, preferred_element_type=F32)
                if own:
                    colpos = kb * tk + lax.broadcasted_iota(jnp.int32, (1, tk), 1)
                    s = jnp.where(colpos <= rowpos, s, NEG)
                s_ref[h, kb] = s
                t = jnp.maximum(s[:, :half], s[:, half:])
                tmax = t if tmax is None else jnp.maximum(tmax, t)
            m_ref[h] = jnp.maximum(m_ref[h], jnp.max(tmax, axis=1, keepdims=True))

    groups = [list(range(g0, min(nkb, g0 + group))) for g0 in range(0, nkb, group)]
    for ks in groups:
        pl.when(ks[-1] < qi)(functools.partial(score_tiles, [(k, False) for k in ks]))
        for j, k_own in enumerate(ks):
            pl.when(k_own == qi)(functools.partial(score_tiles, [(k, False) for k in ks[:j]] + [(k_own, True)]))

    m = [m_ref[h] for h in range(2)]

    def value_tiles(tiles):
        for h in range(2):
            psum, acc = None, acc_ref[h]
            for kb in tiles:
                p = jnp.exp2(s_ref[h, kb] - m[h])
                t = p[:, :half] + p[:, half:]
                psum = t if psum is None else psum + t
                acc = acc + jnp.dot(p.astype(BF16), vb_ref[kb * tk:(kb + 1) * tk, :], preferred_element_type=F32)
            l_ref[h] = l_ref[h] + jnp.sum(psum, axis=1, keepdims=True)
            acc_ref[h] = acc

    for ks in groups:
        pl.when(ks[-1] <= qi)(functools.partial(value_tiles, ks))
        for j in range(len(ks) - 1):
            pl.when(ks[j] == qi)(functools.partial(value_tiles, ks[:j + 1]))

    return [acc_ref[h] / l_ref[h] for h in range(2)]


def _two_pass_scratch(seq, tq, tk):
    return [
        pltpu.VMEM((seq, LANES), BF16), pltpu.VMEM((seq, LANES), BF16),
        pltpu.VMEM((2, seq // tk, tq, tk), F32),
        pltpu.VMEM((2, tq, 1), F32), pltpu.VMEM((2, tq, 1), F32), pltpu.VMEM((2, tq, LANES), F32),
    ]


def _moba_prompt_body(q_ref, k_ref, v_ref, o_ref, ke_ref, vb_ref, s_ref, m_ref, l_ref, acc_ref, ko_ref, bias_ref, *, seq):
    tq = MOBA_BLOCK
    nb = seq // MOBA_BLOCK
    lo = _lane_lo()
    lane = lax.broadcasted_iota(jnp.int32, (1, LANES), 1)
    k = k_ref[...]
    kblock = lax.broadcasted_iota(jnp.int32, (seq, 1), 0) // MOBA_BLOCK
    ke_ref[...] = jnp.where(lo, k, jnp.where(lane - HEAD_DIM == kblock, 1.0, 0.0)).astype(BF16)
    ko_ref[...] = jnp.where(lo, jnp.where(lane == kblock, 1.0, 0.0), k).astype(BF16)
    vb_ref[...] = v_ref[...].astype(BF16)
    kmean = jnp.concatenate(
        [jnp.mean(k_ref[b * MOBA_BLOCK:(b + 1) * MOBA_BLOCK, :], axis=0, keepdims=True) for b in range(nb)],
        axis=0)
    kcat = jnp.concatenate([jnp.where(lo, kmean, 0.0), jnp.where(lo, 0.0, kmean)], axis=0)

    gate = lax.dot_general(kcat, q_ref[...], _ABT, precision=lax.Precision.HIGHEST, preferred_element_type=F32)
    bl = lax.broadcasted_iota(jnp.int32, (nb, seq), 0)
    own = lax.broadcasted_iota(jnp.int32, (1, seq), 1) // MOBA_BLOCK
    bias_t = []
    for h in range(2):
        gh = gate[h * nb:(h + 1) * nb, :]
        cnt = jnp.zeros((nb, seq), F32)
        for b2 in range(nb):
            row = gh[b2:b2 + 1, :]
            beats = jnp.where(row > gh, 1.0, jnp.where(row == gh, jnp.where(bl > b2, 1.0, 0.0), 0.0))
            cnt = cnt + jnp.where(own > b2, beats, 0.0)
        bias_t.append(jnp.where(bl < own, jnp.where(cnt < MOBA_TOPK, 0.0, NEG), 0.0))
    pad = jnp.zeros((HEAD_DIM - nb, seq), F32)
    bias_t = jnp.concatenate([bias_t[1], pad, bias_t[0], pad], axis=0).astype(BF16)
    eye = (lax.broadcasted_iota(jnp.int32, (tq, tq), 0) == lax.broadcasted_iota(jnp.int32, (tq, tq), 1)).astype(BF16)
    for qb in range(seq // tq):
        bias_ref[qb * tq:(qb + 1) * tq, :] = lax.dot_general(eye, bias_t[:, qb * tq:(qb + 1) * tq], _ABT,
                                                             preferred_element_type=F32)

    def qblock(qi, carry):
        r0 = pl.multiple_of(qi * tq, tq)
        qs = q_ref[pl.ds(r0, tq), :] * SCALE_LOG2E
        bias = bias_ref[pl.ds(r0, tq), :]
        qm = (jnp.where(lo, qs, bias).astype(BF16), jnp.where(lo, bias, qs).astype(BF16))
        rowpos = r0 + lax.broadcasted_iota(jnp.int32, (tq, 1), 0)
        outs = _two_pass_rows(qm, qi, rowpos, (ke_ref, ko_ref), vb_ref, s_ref, m_ref, l_ref, acc_ref,
                              nkb=nb, tk=MOBA_BLOCK)
        o_ref[pl.ds(r0, tq), :] = jnp.where(lo, outs[0], outs[1])
        return carry

    lax.fori_loop(0, seq // tq, qblock, 0)


def _moba_prompt(u3):
    n, seq, _ = u3.shape
    nslab = D_MODEL // LANES
    body = functools.partial(_moba_prompt_body, seq=seq)
    return pl.pallas_call(
        body, grid=(n, nslab),
        in_specs=[
            pl.BlockSpec((None, seq, LANES), lambda b, s: (b, 0, s)),
            pl.BlockSpec((None, seq, LANES), lambda b, s: (b, 0, nslab + s)),
            pl.BlockSpec((None, seq, LANES), lambda b, s: (b, 0, 2 * nslab + s)),
        ],
        out_specs=pl.BlockSpec((None, seq, LANES), lambda b, s: (b, 0, s)),
        out_shape=jax.ShapeDtypeStruct((n, seq, D_MODEL), F32),
        scratch_shapes=_two_pass_scratch(seq, MOBA_BLOCK, MOBA_BLOCK)
        + [pltpu.VMEM((seq, LANES), BF16), pltpu.VMEM((seq, LANES), F32)],
        compiler_params=_params(2), name="moba_prompt",
    )(u3, u3, u3)


def _lambda_value(lq1_ref, lk1_ref, lq2_ref, lk2_ref, lam_init):
    a = jnp.sum(lq1_ref[...] * lk1_ref[...], axis=1, keepdims=True)
    b = jnp.sum(lq2_ref[...] * lk2_ref[...], axis=1, keepdims=True)
    return jnp.exp(a) - jnp.exp(b) + lam_init


def _diff_finish(a1, a2, lam, subln, lam_init):
    o = a1 - lam * a2
    ms = jnp.mean(o * o, axis=-1, keepdims=True)
    return (o * lax.rsqrt(ms + EPS) * subln) * (1.0 - lam_init)


def _diff_prompt_body(q_ref, k_ref, v_ref, lq1_ref, lk1_ref, lq2_ref, lk2_ref, subln_ref, o_ref,
                      kb_ref, vb_ref, s_ref, m_ref, l_ref, acc_ref, *, seq, tq, lam_init):
    lo = _lane_lo()
    kb_ref[...] = k_ref[...].astype(BF16)
    vb_ref[...] = v_ref[...].astype(BF16)
    lam = _lambda_value(lq1_ref, lk1_ref, lq2_ref, lk2_ref, lam_init)
    subln = subln_ref[...]

    def qblock(qi, carry):
        r0 = pl.multiple_of(qi * tq, tq)
        qs = q_ref[pl.ds(r0, tq), :] * SCALE_LOG2E
        qm = (jnp.where(lo, qs, 0.0).astype(BF16), jnp.where(lo, 0.0, qs).astype(BF16))
        rowpos = r0 + lax.broadcasted_iota(jnp.int32, (tq, 1), 0)
        outs = _two_pass_rows(qm, qi, rowpos,
                              kb_ref, vb_ref, s_ref, m_ref, l_ref, acc_ref, nkb=seq // tq, tk=tq)
        o_ref[pl.ds(r0, tq), :] = _diff_finish(outs[0], outs[1], lam, subln, lam_init)
        return carry

    lax.fori_loop(0, seq // tq, qblock, 0)


def _diff_prompt(u3, lams, subln, lam_init, tq=256):
    n, seq, _ = u3.shape
    nslab = D_MODEL // LANES
    body = functools.partial(_diff_prompt_body, seq=seq, tq=tq, lam_init=lam_init)
    vec = pl.BlockSpec((1, HEAD_DIM), lambda b, s: (0, 0))
    return pl.pallas_call(
        body, grid=(n, nslab),
        in_specs=[
            pl.BlockSpec((None, seq, LANES), lambda b, s: (b, 0, s)),
            pl.BlockSpec((None, seq, LANES), lambda b, s: (b, 0, nslab + s)),
            pl.BlockSpec((None, seq, LANES), lambda b, s: (b, 0, 2 * nslab + s)),
            vec, vec, vec, vec,
            pl.BlockSpec((1, LANES), lambda b, s: (0, 0)),
        ],
        out_specs=pl.BlockSpec((None, seq, LANES), lambda b, s: (b, 0, s)),
        out_shape=jax.ShapeDtypeStruct((n, seq, D_MODEL), F32),
        scratch_shapes=_two_pass_scratch(seq, tq, tq),
        compiler_params=_params(2), name="diff_prompt",
    )(u3, u3, u3, *lams, subln)


def _dilated_prompt_body(*refs, seq, unroll):
    qkv_refs = refs[:9]
    o_ref = refs[9]
    qc_ref, kc_ref, vc_ref, ocm_ref, lcm_ref, og_ref, lg_ref = refs[10:]
    lo = _lane_lo()
    blk = 128
    nblk = seq // blk
    ri = lax.broadcasted_iota(jnp.int32, (blk, 2 * blk), 0)
    ci = lax.broadcasted_iota(jnp.int32, (blk, 2 * blk), 1)
    bias_band = jnp.where((ci >= ri) & (ci <= ri + blk), 0.0, NEG)
    bias_own = jnp.where((ci >= blk) & (ci <= ri + blk), 0.0, NEG)
    kc_ref[0:blk, :] = jnp.zeros((blk, LANES), BF16)
    vc_ref[0:blk, :] = jnp.zeros((blk, LANES), BF16)

    def attend(qb, kk, vv, bias):
        outs, lses = [], []
        for h in range(2):
            qm = jnp.where(lo, qb, 0.0) if h == 0 else jnp.where(lo, 0.0, qb)
            s = lax.dot_general(qm.astype(BF16), kk, _ABT, preferred_element_type=F32) + bias
            m = jnp.max(s, axis=1, keepdims=True)
            p = jnp.exp(s - m)
            l = jnp.sum(p, axis=1, keepdims=True)
            outs.append(jnp.dot(p.astype(BF16), vv, preferred_element_type=F32) / l)
            lses.append(m + jnp.log(l))
        return jnp.where(lo, outs[0], outs[1]), jnp.where(lo, lses[0], lses[1])

    for g, (win, dil) in enumerate(C_PATTERNS):
        q_ref, k_ref, v_ref = qkv_refs[3 * g:3 * g + 3]
        ncls = seq // dil
        bpc = ncls // blk
        for r in range(dil):
            rows = pl.ds(r, ncls, stride=dil) if dil > 1 else slice(None)
            qc_ref[r * ncls:(r + 1) * ncls, :] = q_ref[rows, :] * SCALE
            kc_ref[blk + r * ncls:blk + (r + 1) * ncls, :] = k_ref[rows, :].astype(BF16)
            vc_ref[blk + r * ncls:blk + (r + 1) * ncls, :] = v_ref[rows, :].astype(BF16)

        def blocks(it, carry):
            for u in range(unroll):
                gb = it * unroll + u
                r0 = pl.multiple_of(gb * blk, blk)
                bias = jnp.where((gb % bpc) != 0, bias_band, bias_own)
                o, lse = attend(qc_ref[pl.ds(r0, blk), :], kc_ref[pl.ds(r0, 2 * blk), :],
                                vc_ref[pl.ds(r0, 2 * blk), :], bias)
                ocm_ref[pl.ds(r0, blk), :] = o
                lcm_ref[pl.ds(r0, blk), :] = lse
            return carry

        lax.fori_loop(0, nblk // unroll, blocks, 0)
        for r in range(dil):
            rows = pl.ds(r, ncls, stride=dil) if dil > 1 else slice(None)
            src = slice(r * ncls, (r + 1) * ncls)
            og_ref[g, rows, :] = ocm_ref[src, :]
            lg_ref[g, rows, :] = lcm_ref[src, :]

    lmax = jnp.maximum(jnp.maximum(lg_ref[0], lg_ref[1]), lg_ref[2])
    w = [jnp.exp(lg_ref[g] - lmax) for g in range(3)]
    den = w[0] + w[1] + w[2]
    o_ref[...] = (w[0] * og_ref[0] + w[1] * og_ref[1] + w[2] * og_ref[2]) / den


def _dilated_prompt(u3):
    n, seq, _ = u3.shape
    nslab = D_MODEL // LANES
    in_specs = []
    for g in range(3):
        for c in range(3):
            base = (3 * g + c) * nslab
            in_specs.append(pl.BlockSpec((None, seq, LANES), lambda b, s, base=base: (b, 0, base + s)))
    body = functools.partial(_dilated_prompt_body, seq=seq, unroll=4)
    return pl.pallas_call(
        body, grid=(n, nslab),
        in_specs=in_specs,
        out_specs=pl.BlockSpec((None, seq, LANES), lambda b, s: (b, 0, s)),
        out_shape=jax.ShapeDtypeStruct((n, seq, D_MODEL), F32),
        scratch_shapes=[
            pltpu.VMEM((seq, LANES), F32), pltpu.VMEM((seq + 128, LANES), BF16), pltpu.VMEM((seq + 128, LANES), BF16),
            pltpu.VMEM((seq, LANES), F32), pltpu.VMEM((seq, LANES), F32),
            pltpu.VMEM((3, seq, LANES), F32), pltpu.VMEM((3, seq, LANES), F32),
        ],
        compiler_params=_params(2), name="dilated_prompt",
    )(*([u3] * 9))


def _kmean_body(pt_ref, *refs, pages_per_block, pps):
    k_refs, o_ref = refs[:pps], refs[pps]
    p = pl.program_id(1)
    heads, dim, page = k_refs[0].shape
    ones = jnp.ones((8, page), F32)
    bps = pps // pages_per_block
    for b in range(bps):
        s = k_refs[b * pages_per_block][...]
        for i in range(1, pages_per_block):
            s = s + k_refs[b * pages_per_block + i][...]
        r = lax.dot_general(ones, s.reshape(heads * dim, page), _ABT, precision=lax.Precision.HIGHEST,
                            preferred_element_type=F32)
        o_ref[pl.ds(p * bps + b, 1), :] = r[0:1, :] * (1.0 / (pages_per_block * page))


def _moba_kmean(cache_t, layer, page_table, pps=8):
    n_dec, n_pages = page_table.shape
    _, _, heads, dim, page = cache_t.shape
    ppb = MOBA_BLOCK // page
    assert pps % ppb == 0 and n_pages % pps == 0
    pages = [pl.BlockSpec((None, None, heads, dim, page), lambda b, p, pt, i=i: (layer, pt[b, pps * p + i], 0, 0, 0))
             for i in range(pps)]
    grid_spec = pltpu.PrefetchScalarGridSpec(
        num_scalar_prefetch=1, grid=(n_dec, n_pages // pps),
        in_specs=pages,
        out_specs=pl.BlockSpec((None, n_pages // ppb, heads * dim), lambda b, p, pt: (b, 0, 0)),
    )
    return pl.pallas_call(
        functools.partial(_kmean_body, pages_per_block=ppb, pps=pps), grid_spec=grid_spec,
        out_shape=jax.ShapeDtypeStruct((n_dec, n_pages // ppb, heads * dim), F32),
        compiler_params=_params(2), name="moba_kmean",
    )(page_table, *([cache_t] * pps))


def _moba_select_body(km_ref, q_ref, seg_ref, o_ref):
    nbk = km_ref.shape[0]
    gate = jnp.dot(km_ref[...] * q_ref[...], seg_ref[...], precision=lax.Precision.HIGHEST,
                   preferred_element_type=F32)
    rows = lax.broadcasted_iota(jnp.int32, (nbk, LANES), 0)
    o_ref[...] = jnp.zeros(o_ref.shape, jnp.int32)
    for j in range(MOBA_TOPK):
        mx = jnp.max(gate, axis=0, keepdims=True)
        idx = jnp.min(jnp.where(gate == mx, rows, nbk), axis=0, keepdims=True)
        o_ref[j:j + 1, :] = idx
        gate = jnp.where(rows == idx, -jnp.inf, gate)


def _moba_select(kmean, q3, seg):
    n_dec, nbk, _ = kmean.shape
    return pl.pallas_call(
        _moba_select_body, grid=(n_dec,),
        in_specs=[
            pl.BlockSpec((None, nbk, D_MODEL), lambda b: (b, 0, 0)),
            pl.BlockSpec((None, 1, D_MODEL), lambda b: (b, 0, 0)),
            pl.BlockSpec((D_MODEL, LANES), lambda b: (0, 0)),
        ],
        out_specs=pl.BlockSpec((None, 8, LANES), lambda b: (b, 0, 0)),
        out_shape=jax.ShapeDtypeStruct((n_dec, 8, LANES), jnp.int32),
        compiler_params=_params(1), name="moba_select",
    )(kmean, q3, seg)


def _moba_sample_body(pt_ref, sel_ref, q_ref, kn_ref, vn_ref, kpool, vpool, o_ref, kbuf, vbuf, sem,
                      *, layer, n_heads, ppb):
    b = pl.program_id(0)
    n_chunks = MOBA_TOPK * ppb
    slot = b % 2

    def copies(bb, sl):
        out = []
        for h in range(n_heads):
            for c in range(n_chunks):
                pg = pt_ref[bb, ppb * sel_ref[bb, (c // ppb) * n_heads + h] + c % ppb]
                out.append(pltpu.make_async_copy(kpool.at[layer, pg, h], kbuf.at[sl, h, c], sem.at[0, sl]))
                out.append(pltpu.make_async_copy(vpool.at[layer, pg, h], vbuf.at[sl, h, c], sem.at[1, sl]))
        return out

    @pl.when(b == 0)
    def _():
        for cp in copies(b, slot):
            cp.start()

    @pl.when(b + 1 < pl.num_programs(0))
    def _():
        for cp in copies(b + 1, 1 - slot):
            cp.start()

    for cp in copies(b, slot):
        cp.wait()

    q = q_ref[...] * SCALE
    s_self = jnp.sum(q * kn_ref[...], axis=-1, keepdims=True)
    for h in range(n_heads):
        qh = jnp.broadcast_to(q[h:h + 1, :], (8, q.shape[1])).astype(BF16)
        s = jnp.concatenate([jnp.dot(qh, kbuf[slot, h, c].astype(BF16), preferred_element_type=F32)
                             for c in range(n_chunks)], axis=1)
        ss = s_self[h:h + 1, :]
        m = jnp.maximum(jnp.max(s, axis=1, keepdims=True), ss)
        p = jnp.exp(s - m)
        ps = jnp.exp(ss - m)
        den = jnp.sum(p, axis=1, keepdims=True) + ps
        pb = p.astype(BF16)
        page = kbuf.shape[-1]
        o = ps * vn_ref[h:h + 1, :]
        for c in range(n_chunks):
            o = o + lax.dot_general(pb[:, c * page:(c + 1) * page], vbuf[slot, h, c].astype(BF16), _ABT,
                                    preferred_element_type=F32)
        o_ref[h:h + 1, :] = (o / den)[0:1, :]


def _moba_sample(cache_kt, cache_vt, layer, page_table, sel, q, kn, vn):
    n_dec = page_table.shape[0]
    _, _, heads, dim, page = cache_kt.shape
    ppb = MOBA_BLOCK // page
    n_chunks = MOBA_TOPK * ppb
    row = pl.BlockSpec((None, heads, dim), lambda b, pt, idx: (b, 0, 0))
    grid_spec = pltpu.PrefetchScalarGridSpec(
        num_scalar_prefetch=2, grid=(n_dec,),
        in_specs=[row, row, row, pl.BlockSpec(memory_space=pl.ANY), pl.BlockSpec(memory_space=pl.ANY)],
        out_specs=row,
        scratch_shapes=[pltpu.VMEM((2, heads, n_chunks, dim, page), F32),
                        pltpu.VMEM((2, heads, n_chunks, dim, page), F32),
                        pltpu.SemaphoreType.DMA((2, 2))],
    )
    out = pl.pallas_call(
        functools.partial(_moba_sample_body, layer=layer, n_heads=heads, ppb=ppb), grid_spec=grid_spec,
        out_shape=jax.ShapeDtypeStruct((n_dec, heads, dim), F32),
        compiler_params=_params(1), name="moba_sample",
    )(page_table, sel, q, kn, vn, cache_kt, cache_vt)
    return out.reshape(n_dec, heads * dim)


def _diff_sample_body(pt_ref, q_ref, kn_ref, vn_ref, lq1_ref, lk1_ref, lq2_ref, lk2_ref, subln_ref, *refs,
                      n_heads, page, pps, lam_init):
    k_refs, v_refs = refs[:pps], refs[pps:2 * pps]
    o_ref, m_ref, l_ref, acc_ref = refs[2 * pps:]
    pg = pl.program_id(1)
    nrow = 2 * n_heads
    row = lax.broadcasted_iota(jnp.int32, (nrow, LANES), 0)
    lane = lax.broadcasted_iota(jnp.int32, (nrow, LANES), 1)
    rowcol = lax.broadcasted_iota(jnp.int32, (nrow, 1), 0)

    @pl.when(pg == 0)
    def _():
        m_ref[...] = jnp.full(m_ref.shape, NEG, F32)
        l_ref[...] = jnp.zeros(l_ref.shape, F32)
        acc_ref[...] = jnp.zeros(acc_ref.shape, F32)

    lhs = [jnp.where(row == 2 * h + lane // HEAD_DIM, q_ref[h:h + 1, :] * SCALE, 0.0) for h in range(n_heads)]
    lhs_bf = [x.astype(BF16) for x in lhs]
    parts = []
    for i in range(pps):
        s = None
        for h in range(n_heads):
            kh = k_refs[i][pl.ds(h, page, stride=n_heads), :].astype(BF16)
            d = lax.dot_general(lhs_bf[h], kh, _ABT, preferred_element_type=F32)
            s = d if s is None else s + d
        parts.append(s)
    s = jnp.concatenate(parts, axis=1)
    m_old = m_ref[...]
    m_new = jnp.maximum(m_old, jnp.max(s, axis=1, keepdims=True))
    alpha = jnp.exp(m_old - m_new)
    p = jnp.exp(s - m_new)
    l_ref[...] = alpha * l_ref[...] + jnp.sum(p, axis=1, keepdims=True)
    m_ref[...] = m_new
    acc = alpha * acc_ref[...]
    for i in range(pps):
        pi = p[:, i * page:(i + 1) * page]
        for h in range(n_heads):
            ph = jnp.where(rowcol // 2 == h, pi, 0.0).astype(BF16)
            vh = v_refs[i][pl.ds(h, page, stride=n_heads), :].astype(BF16)
            acc = acc + jnp.dot(ph, vh, preferred_element_type=F32)
    acc_ref[...] = acc

    @pl.when(pg == pl.num_programs(1) - 1)
    def _():
        lam = _lambda_value(lq1_ref, lk1_ref, lq2_ref, lk2_ref, lam_init)
        q_all, kn_all, vn_all = lhs[0], None, None
        for h in range(1, n_heads):
            q_all = q_all + lhs[h]
        for h in range(n_heads):
            sel = rowcol // 2 == h
            kn_h = jnp.where(sel, kn_ref[h:h + 1, :], 0.0)
            vn_h = jnp.where(sel, vn_ref[h:h + 1, :], 0.0)
            kn_all = kn_h if kn_all is None else kn_all + kn_h
            vn_all = vn_h if vn_all is None else vn_all + vn_h
        s_self = jnp.sum(q_all * kn_all, axis=1, keepdims=True)
        m1 = m_ref[...]
        m2 = jnp.maximum(m1, s_self)
        a1 = jnp.exp(m1 - m2)
        p_self = jnp.exp(s_self - m2)
        a = (a1 * acc_ref[...] + p_self * vn_all) / (a1 * l_ref[...] + p_self)
        for h in range(n_heads):
            o_ref[h:h + 1, :] = _diff_finish(a[2 * h:2 * h + 1, :], a[2 * h + 1:2 * h + 2, :], lam,
                                             subln_ref[...], lam_init)


def _diff_sample(pool_k, pool_v, page_table, us4, lams, subln, lam_init, pps=4):
    n_dec, n_pages = page_table.shape
    n_heads = us4.shape[2]
    rows = pool_k.shape[1]
    assert n_pages % pps == 0
    vec = pl.BlockSpec((1, HEAD_DIM), lambda b, p, pt: (0, 0))
    pages = [pl.BlockSpec((None, rows, LANES), lambda b, p, pt, i=i: (pt[b, pps * p + i], 0, 0)) for i in range(pps)]
    grid_spec = pltpu.PrefetchScalarGridSpec(
        num_scalar_prefetch=1, grid=(n_dec, n_pages // pps),
        in_specs=[
            pl.BlockSpec((None, None, n_heads, LANES), lambda b, p, pt: (b, 0, 0, 0)),
            pl.BlockSpec((None, None, n_heads, LANES), lambda b, p, pt: (b, 1, 0, 0)),
            pl.BlockSpec((None, None, n_heads, LANES), lambda b, p, pt: (b, 2, 0, 0)),
            vec, vec, vec, vec,
            pl.BlockSpec((1, LANES), lambda b, p, pt: (0, 0)),
        ] + pages + pages,
        out_specs=pl.BlockSpec((None, n_heads, LANES), lambda b, p, pt: (b, 0, 0)),
        scratch_shapes=[pltpu.VMEM((2 * n_heads, 1), F32), pltpu.VMEM((2 * n_heads, 1), F32),
                        pltpu.VMEM((2 * n_heads, LANES), F32)],
    )
    body = functools.partial(_diff_sample_body, n_heads=n_heads, page=rows // n_heads, pps=pps, lam_init=lam_init)
    out = pl.pallas_call(
        body, grid_spec=grid_spec,
        out_shape=jax.ShapeDtypeStruct((n_dec, n_heads, LANES), F32),
        compiler_params=_params(2), name="diff_sample",
    )(page_table, us4, us4, us4, *lams, subln, *([pool_k] * pps), *([pool_v] * pps))
    return out.reshape(n_dec, D_MODEL)


def _dilated_sample_body(qkv_ref, new_ref, c0_ref, c1_ref, c2_ref, o_ref, n0_ref, n1_ref, n2_ref, *, hps):
    b = pl.program_id(0)
    c_refs = (c0_ref, c1_ref, c2_ref)
    n_refs = (n0_ref, n1_ref, n2_ref)
    n_lane = lax.broadcasted_iota(jnp.int32, (1, new_ref.shape[-1]), 1)
    for h in range(hps):
        ms, ls, accs = [], [], []
        for g, (win, dil) in enumerate(C_PATTERNS):
            rows = c_refs[g].shape[-1]
            q = jnp.broadcast_to(qkv_ref[3 * g, h:h + 1, :] * SCALE, (8, HEAD_DIM))
            kn = qkv_ref[3 * g + 1, h:h + 1, :]
            vn = qkv_ref[3 * g + 2, h:h + 1, :]
            kt = c_refs[g][0, h]
            vt = c_refs[g][1, h]
            lane = lax.broadcasted_iota(jnp.int32, (1, rows), 1)
            sc = jnp.dot(q.astype(BF16), kt.astype(BF16), preferred_element_type=F32)
            if dil > 1:
                sc = jnp.where(lane % dil == 0, sc, NEG)
            s_self = jnp.sum(q * kn, axis=1, keepdims=True)
            m = jnp.maximum(jnp.max(sc, axis=1, keepdims=True), s_self)
            p = jnp.exp(sc - m)
            p_self = jnp.exp(s_self - m)
            ms.append(m)
            ls.append(jnp.sum(p, axis=1, keepdims=True) + p_self)
            accs.append(lax.dot_general(p.astype(BF16), vt.astype(BF16), _ABT, preferred_element_type=F32)
                        + p_self * vn)
            for kv, x in ((0, kt), (1, vt)):
                col = jnp.sum(jnp.where(n_lane == b, new_ref[g, kv, h], 0.0), axis=1, keepdims=True)
                n_refs[g][kv, h] = jnp.where(lane == rows - 1, col, pltpu.roll(x, rows - 1, 1))
        mx = jnp.maximum(jnp.maximum(ms[0], ms[1]), ms[2])
        w = [jnp.exp(ms[g] - mx) for g in range(3)]
        den = w[0] * ls[0] + w[1] * ls[1] + w[2] * ls[2]
        o = (w[0] * accs[0] + w[1] * accs[1] + w[2] * accs[2]) / den
        o_ref[h:h + 1, :] = o[0:1, :]


def _dilated_sample(qkv, new_t, caches_t, hps=4):
    n_dec = qkv.shape[0]
    heads = caches_t[0].shape[2]
    hb = heads // hps
    in_specs = [
        pl.BlockSpec((None, 9, None, hps, HEAD_DIM), lambda b, s: (b, 0, s, 0, 0)),
        pl.BlockSpec((3, 2, hps, HEAD_DIM, n_dec), lambda b, s: (0, 0, s, 0, 0)),
    ]
    cache_specs = [pl.BlockSpec((None, 2, hps, HEAD_DIM, c.shape[-1]), lambda b, s: (b, 0, s, 0, 0)) for c in caches_t]
    outs = pl.pallas_call(
        functools.partial(_dilated_sample_body, hps=hps), grid=(n_dec, hb),
        in_specs=in_specs + cache_specs,
        out_specs=[pl.BlockSpec((None, None, hps, HEAD_DIM), lambda b, s: (b, s, 0, 0))] + cache_specs,
        out_shape=[jax.ShapeDtypeStruct((n_dec, hb, hps, HEAD_DIM), F32)]
        + [jax.ShapeDtypeStruct(c.shape, F32) for c in caches_t],
        compiler_params=_params(2), name="dilated_sample",
    )(qkv, new_t, *caches_t)
    return outs[0].reshape(n_dec, heads * HEAD_DIM), outs[1:]


def _rope_tables(pos):
    half = HEAD_DIM // 2
    inv = ROPE_THETA ** (-jnp.arange(half, dtype=F32) / half)
    ang = pos.astype(F32)[:, None] * inv[None, :]
    cos, sin = jnp.cos(ang), jnp.sin(ang)
    cos = jnp.concatenate([cos, cos], axis=1)
    sin = jnp.concatenate([-sin, sin], axis=1)
    return jnp.tile(cos, (1, LANES // HEAD_DIM)), jnp.tile(sin, (1, LANES // HEAD_DIM))


def _head_gain(g):
    return jnp.tile(g.astype(F32), D_MODEL // HEAD_DIM)


def kernel(x_prompt, x_sample, cache_a_k, cache_a_v, cache_b_k, cache_b_v, cache_c_kv0, cache_c_kv1, cache_c_kv2,
           page_table, norm_a, w_in_a, qn_a, kn_a, w_out_a, norm_b, w_in_b, qn_b, kn_b, lam_q1_b, lam_k1_b,
           lam_q2_b, lam_k2_b, subln_b, w_out_b, norm_c, w_in_c, qn_c, kn_c, w_out_c):
    n, seq, _ = x_prompt.shape
    n_dec, dec_seq, _ = x_sample.shape
    assert dec_seq == 1 and seq % 512 == 0
    page = cache_a_k.shape[2]
    n_pages = page_table.shape[1]
    past = n_pages * page
    n_pool = cache_a_k.shape[1]
    depth = norm_a.shape[0] + norm_b.shape[0] + norm_c.shape[0]
    tm = 1024
    a_heads = D_MODEL // HEAD_DIM
    b_heads = D_MODEL // (2 * HEAD_DIM)
    c_caches = (cache_c_kv0, cache_c_kv1, cache_c_kv2)

    cos_p, sin_p = _rope_tables(jnp.arange(seq, dtype=jnp.int32))
    cos_s, sin_s = _rope_tables(jnp.full((n_dec,), past, jnp.int32))
    bd = (jnp.arange(256)[:, None] // HEAD_DIM == jnp.arange(256)[None, :] // HEAD_DIM).astype(BF16)
    zero_gain = jnp.zeros((D_MODEL,), F32)
    seg = (jnp.arange(D_MODEL)[:, None] // HEAD_DIM == jnp.arange(LANES)[None, :]).astype(F32)
    cache_a_kt = jnp.transpose(cache_a_k, (0, 1, 3, 4, 2))
    cache_a_vt = jnp.transpose(cache_a_v, (0, 1, 3, 4, 2))

    hp = x_prompt.reshape(n * seq, D_MODEL)
    hs = x_sample.reshape(n_dec, D_MODEL)
    ak_p, av_p, ak_s, av_s = [], [], [], []
    bk_p, bv_p, bk_s, bv_s = [], [], [], []
    ckv_p = [[] for _ in C_PATTERNS]
    ckv_s = [[] for _ in C_PATTERNS]

    def project_both(norm, w_in, gains, modes):
        w_bf = w_in.astype(BF16)
        gains = gains.reshape(gains.shape[0], 1, COL_TILE)
        up = _project(hp, norm, w_bf, gains, modes, cos_p, sin_p, bd, tm)
        us = _project(hs, norm, w_bf, gains, modes, cos_s, sin_s, bd, n_dec)
        return up, us

    for layer in range(depth):
        kind, j = layer % 3, layer // 3
        if kind == 0:
            gains = jnp.stack([_head_gain(qn_a[j]), _head_gain(kn_a[j]), zero_gain, zero_gain])
            modes = jnp.array([1, 1, 0, 0], jnp.int32)
            up, us = project_both(norm_a[j], w_in_a[j], gains, modes)
            op = _moba_prompt(up.reshape(n, seq, 4 * D_MODEL)).reshape(n * seq, D_MODEL)
            us3 = us.reshape(n_dec, 4, a_heads, HEAD_DIM)
            kmean = _moba_kmean(cache_a_kt, j, page_table)
            sel = _moba_select(kmean, us.reshape(n_dec, 1, 4 * D_MODEL), seg)
            sel = sel[:, :MOBA_TOPK, :a_heads].reshape(n_dec, MOBA_TOPK * a_heads)
            os_ = _moba_sample(cache_a_kt, cache_a_vt, j, page_table, sel, us3[:, 0], us3[:, 1], us3[:, 2])
            hp = _out_project(op, up, hp, w_out_a[j].astype(BF16), tm)
            hs = _out_project(os_, us, hs, w_out_a[j].astype(BF16), n_dec)
            ak_p.append(up[:, D_MODEL:2 * D_MODEL].reshape(n, seq, a_heads, HEAD_DIM))
            av_p.append(up[:, 2 * D_MODEL:3 * D_MODEL].reshape(n, seq, a_heads, HEAD_DIM))
            ak_s.append(us[:, D_MODEL:2 * D_MODEL].reshape(n_dec, 1, a_heads, HEAD_DIM))
            av_s.append(us[:, 2 * D_MODEL:3 * D_MODEL].reshape(n_dec, 1, a_heads, HEAD_DIM))
        elif kind == 1:
            lam_init = 0.8 - 0.6 * math.exp(-0.3 * layer)
            gains = jnp.stack([_head_gain(qn_b[j]), _head_gain(kn_b[j]), zero_gain, zero_gain])
            modes = jnp.array([1, 1, 0, 0], jnp.int32)
            up, us = project_both(norm_b[j], w_in_b[j], gains, modes)
            lams = tuple(v[j].astype(F32).reshape(1, HEAD_DIM) for v in (lam_q1_b, lam_k1_b, lam_q2_b, lam_k2_b))
            subln = subln_b[j].astype(F32).reshape(1, LANES)
            op = _diff_prompt(up.reshape(n, seq, 4 * D_MODEL), lams, subln, lam_init).reshape(n * seq, D_MODEL)
            pool_k = cache_b_k[j].reshape(n_pool, page * b_heads, LANES)
            pool_v = cache_b_v[j].reshape(n_pool, page * b_heads, LANES)
            os_ = _diff_sample(pool_k, pool_v, page_table, us.reshape(n_dec, 4, b_heads, LANES), lams, subln, lam_init)
            hp = _out_project(op, up, hp, w_out_b[j].astype(BF16), tm)
            hs = _out_project(os_, us, hs, w_out_b[j].astype(BF16), n_dec)
            bk_p.append(up[:, D_MODEL:2 * D_MODEL].reshape(n, seq, b_heads, LANES))
            bv_p.append(up[:, 2 * D_MODEL:3 * D_MODEL].reshape(n, seq, b_heads, LANES))
            bk_s.append(us[:, D_MODEL:2 * D_MODEL].reshape(n_dec, 1, b_heads, LANES))
            bv_s.append(us[:, 2 * D_MODEL:3 * D_MODEL].reshape(n_dec, 1, b_heads, LANES))
        else:
            gain_rows, mode_vals = [], []
            for g in range(len(C_PATTERNS)):
                gain_rows += [_head_gain(qn_c[j, g]), _head_gain(kn_c[j, g]), zero_gain]
                mode_vals += [1, 1, 0]
            gains = jnp.stack(gain_rows + [zero_gain])
            modes = jnp.array(mode_vals + [0], jnp.int32)
            up, us = project_both(norm_c[j], w_in_c[j], gains, modes)
            e = up.shape[1]
            op = _dilated_prompt(up.reshape(n, seq, e)).reshape(n * seq, D_MODEL)
            n_grp = len(C_PATTERNS)
            hps = 4
            qkv = us[:, :3 * n_grp * D_MODEL].reshape(n_dec, 3 * n_grp, a_heads // hps, hps, HEAD_DIM)
            new_kv = us[:, :3 * n_grp * D_MODEL].reshape(n_dec, n_grp, 3, a_heads, HEAD_DIM)[:, :, 1:]
            new_t = jnp.transpose(new_kv, (1, 2, 3, 4, 0))
            caches_t = []
            for g, (win, dil) in enumerate(C_PATTERNS):
                assert c_caches[g].shape[2] == win and win // dil == 128
                caches_t.append(jnp.transpose(c_caches[g][j], (0, 2, 3, 4, 1)))
            os_, adv = _dilated_sample(qkv, new_t, caches_t, hps)
            hp = _out_project(op, up, hp, w_out_c[j].astype(BF16), tm)
            hs = _out_project(os_, us, hs, w_out_c[j].astype(BF16), n_dec)
            up3 = up.reshape(n, seq, e)
            for g, (win, dil) in enumerate(C_PATTERNS):
                c0 = (3 * g + 1) * D_MODEL
                keep_p = min(win, seq)
                ckv_p[g].append(up3[:, seq - keep_p:, c0:c0 + 2 * D_MODEL].reshape(n, keep_p, 2, a_heads, HEAD_DIM))
                ckv_s[g].append(jnp.transpose(adv[g], (0, 4, 1, 2, 3)))

    return (hp.reshape(n, seq, D_MODEL), hs.reshape(n_dec, 1, D_MODEL),
            jnp.stack(ak_p), jnp.stack(av_p), jnp.stack(ak_s), jnp.stack(av_s),
            jnp.stack(bk_p), jnp.stack(bv_p), jnp.stack(bk_s), jnp.stack(bv_s),
            jnp.stack(ckv_p[0]), jnp.stack(ckv_p[1]), jnp.stack(ckv_p[2]),
            jnp.stack(ckv_s[0]), jnp.stack(ckv_s[1]), jnp.stack(ckv_s[2]))
```

```python
import functools
import math

import jax
import jax.numpy as jnp
from jax import lax
from jax.experimental import pallas as pl
from jax.experimental.pallas import tpu as pltpu

F32 = jnp.float32
BF16 = jnp.bfloat16

D_MODEL = 1024
HEAD_DIM = 64
LANES = 128
COL_TILE = 1024
EPS = 1e-6
ROPE_THETA = 10000.0
SCALE = HEAD_DIM ** -0.5
SCALE_LOG2E = SCALE * math.log2(math.e)
NEG = -1e30
MOBA_BLOCK = 256
MOBA_TOPK = 3
C_PATTERNS = ((128, 1), (512, 4), (2048, 16))
VMEM_LIMIT = 48 * 1024 * 1024

_ABT = (((1,), (1,)), ((), ()))


def _params(n_axes):
    return pltpu.CompilerParams(dimension_semantics=("arbitrary",) * n_axes,
                                vmem_limit_bytes=VMEM_LIMIT)


def _lane_lo():
    return lax.broadcasted_iota(jnp.int32, (1, LANES), 1) < HEAD_DIM


def _row_picks_head():
    row = lax.broadcasted_iota(jnp.int32, (8, LANES), 0)
    lane = lax.broadcasted_iota(jnp.int32, (8, LANES), 1)
    return (lane // HEAD_DIM) == row


def _proj_body(mode_ref, x_ref, g_ref, w_ref, gain_ref, cos_ref, sin_ref, bd_ref, o_ref, xn_ref):
    j = pl.program_id(1)

    @pl.when(j == 0)
    def _():
        x = x_ref[...]
        ms = jnp.mean(x * x, axis=-1, keepdims=True)
        xn_ref[...] = (x * lax.rsqrt(ms + EPS) * g_ref[...]).astype(BF16)

    u = jnp.dot(xn_ref[...], w_ref[...], preferred_element_type=F32)
    mode = mode_ref[j]

    @pl.when(mode == 0)
    def _():
        o_ref[...] = u

    @pl.when(mode == 1)
    def _():
        gain = gain_ref[...]
        cos = cos_ref[...]
        sin = sin_ref[...]
        lane = lax.broadcasted_iota(jnp.int32, (1, LANES), 1)
        first_half = (lane % HEAD_DIM) < (HEAD_DIM // 2)
        for c in range(COL_TILE // 256):
            uc = u[:, c * 256:(c + 1) * 256]
            ss = jnp.dot((uc * uc).astype(BF16), bd_ref[...], preferred_element_type=F32)
            y = uc * lax.rsqrt(ss * (1.0 / HEAD_DIM) + EPS) * gain[:, c * 256:(c + 1) * 256]
            for s in range(2):
                ys = y[:, s * LANES:(s + 1) * LANES]
                rot = jnp.where(first_half, pltpu.roll(ys, 96, 1), pltpu.roll(ys, 32, 1))
                o_ref[:, c * 256 + s * LANES:c * 256 + (s + 1) * LANES] = ys * cos + rot * sin


def _project(x, g, w_bf, gains, modes, cos, sin, bd, tm):
    t, e = x.shape[0], w_bf.shape[1]
    nt = e // COL_TILE
    npb = cos.shape[0] // tm
    grid_spec = pltpu.PrefetchScalarGridSpec(
        num_scalar_prefetch=1,
        grid=(t // tm, nt),
        in_specs=[
            pl.BlockSpec((tm, D_MODEL), lambda i, j, m: (i, 0)),
            pl.BlockSpec((1, D_MODEL), lambda i, j, m: (0, 0)),
            pl.BlockSpec((D_MODEL, COL_TILE), lambda i, j, m: (0, j)),
            pl.BlockSpec((None, 1, COL_TILE), lambda i, j, m: (j, 0, 0)),
            pl.BlockSpec((tm, LANES), lambda i, j, m: (i % npb, 0)),
            pl.BlockSpec((tm, LANES), lambda i, j, m: (i % npb, 0)),
            pl.BlockSpec((256, 256), lambda i, j, m: (0, 0)),
        ],
        out_specs=pl.BlockSpec((tm, COL_TILE), lambda i, j, m: (i, j)),
        scratch_shapes=[pltpu.VMEM((tm, D_MODEL), BF16)],
    )
    return pl.pallas_call(
        _proj_body, grid_spec=grid_spec,
        out_shape=jax.ShapeDtypeStruct((t, e), F32),
        compiler_params=_params(2), name="proj_in",
    )(modes, x, g.reshape(1, D_MODEL), w_bf, gains, cos, sin, bd)


def _outproj_body(o_ref, gate_ref, h_ref, w_ref, y_ref):
    g = gate_ref[...]
    z = o_ref[...] * (g * jax.nn.sigmoid(g))
    y_ref[...] = h_ref[...] + jnp.dot(z.astype(BF16), w_ref[...], preferred_element_type=F32)


def _out_project(o, u, h, w_bf, tm):
    t = o.shape[0]
    gate_blk = u.shape[1] // COL_TILE - 1
    return pl.pallas_call(
        _outproj_body, grid=(t // tm,),
        in_specs=[
            pl.BlockSpec((tm, D_MODEL), lambda i: (i, 0)),
            pl.BlockSpec((tm, COL_TILE), lambda i: (i, gate_blk)),
            pl.BlockSpec((tm, D_MODEL), lambda i: (i, 0)),
            pl.BlockSpec((D_MODEL, D_MODEL), lambda i: (0, 0)),
        ],
        out_specs=pl.BlockSpec((tm, D_MODEL), lambda i: (i, 0)),
        out_shape=jax.ShapeDtypeStruct((t, D_MODEL), F32),
        compiler_params=_params(1), name="proj_out",
    )(o, u, h, w_bf)


def _two_pass_rows(qm, qi, rowpos, kb_ref, vb_ref, s_ref, m_ref, l_ref, acc_ref, *, nkb, tk, group=8):
    half = tk // 2
    m_ref[...] = jnp.full(m_ref.shape, NEG, F32)
    l_ref[...] = jnp.zeros(l_ref.shape, F32)
    acc_ref[...] = jnp.zeros(acc_ref.shape, F32)

    def score_tiles(tiles):
        for h in range(2):
            tmax = None
            for kb, own in tiles:
                k_ref = kb_ref[h] if isinstance(kb_ref, (tuple, list)) else kb_ref
                s = lax.dot_general(qm[h], k_ref[kb * tk:(kb + 1) * tk, :], _ABT, preferred_element_type=F32)
                if own:
                    colpos = kb * tk + lax.broadcasted_iota(jnp.int32, (1, tk), 1)
                    s = jnp.where(colpos <= rowpos, s, NEG)
                s_ref[h, kb] = s
                t = jnp.maximum(s[:, :half], s[:, half:])
                tmax = t if tmax is None else jnp.maximum(tmax, t)
            m_ref[h] = jnp.maximum(m_ref[h], jnp.max(tmax, axis=1, keepdims=True))

    groups = [list(range(g0, min(nkb, g0 + group))) for g0 in range(0, nkb, group)]
    for ks in groups:
        pl.when(ks[-1] < qi)(functools.partial(score_tiles, [(k, False) for k in ks]))
        for j, k_own in enumerate(ks):
            pl.when(k_own == qi)(functools.partial(score_tiles, [(k, False) for k in ks[:j]] + [(k_own, True)]))

    m = [m_ref[h] for h in range(2)]

    def value_tiles(tiles):
        for h in range(2):
            psum, acc = None, acc_ref[h]
            for kb in tiles:
                p = jnp.exp2(s_ref[h, kb] - m[h])
                t = p[:, :half] + p[:, half:]
                psum = t if psum is None else psum + t
                acc = acc + jnp.dot(p.astype(BF16), vb_ref[kb * tk:(kb + 1) * tk, :], preferred_element_type=F32)
            l_ref[h] = l_ref[h] + jnp.sum(psum, axis=1, keepdims=True)
            acc_ref[h] = acc

    for ks in groups:
        pl.when(ks[-1] <= qi)(functools.partial(value_tiles, ks))
        for j in range(len(ks) - 1):
            pl.when(ks[j] == qi)(functools.partial(value_tiles, ks[:j + 1]))

    return [acc_ref[h] / l_ref[h] for h in range(2)]


def _two_pass_scratch(seq, tq, tk):
    return [
        pltpu.VMEM((seq, LANES), BF16), pltpu.VMEM((seq, LANES), BF16),
        pltpu.VMEM((2, seq // tk, tq, tk), F32),
        pltpu.VMEM((2, tq, 1), F32), pltpu.VMEM((2, tq, 1), F32), pltpu.VMEM((2, tq, LANES), F32),
    ]


def _moba_prompt_body(q_ref, k_ref, v_ref, o_ref, ke_ref, vb_ref, s_ref, m_ref, l_ref, acc_ref, ko_ref, bias_ref, *, seq):
    tq = MOBA_BLOCK
    nb = seq // MOBA_BLOCK
    lo = _lane_lo()
    lane = lax.broadcasted_iota(jnp.int32, (1, LANES), 1)
    k = k_ref[...]
    kblock = lax.broadcasted_iota(jnp.int32, (seq, 1), 0) // MOBA_BLOCK
    ke_ref[...] = jnp.where(lo, k, jnp.where(lane - HEAD_DIM == kblock, 1.0, 0.0)).astype(BF16)
    ko_ref[...] = jnp.where(lo, jnp.where(lane == kblock, 1.0, 0.0), k).astype(BF16)
    vb_ref[...] = v_ref[...].astype(BF16)
    kmean = jnp.concatenate(
        [jnp.mean(k_ref[b * MOBA_BLOCK:(b + 1) * MOBA_BLOCK, :], axis=0, keepdims=True) for b in range(nb)],
        axis=0)
    kcat = jnp.concatenate([jnp.where(lo, kmean, 0.0), jnp.where(lo, 0.0, kmean)], axis=0)

    gate = lax.dot_general(kcat, q_ref[...], _ABT, precision=lax.Precision.HIGHEST, preferred_element_type=F32)
    bl = lax.broadcasted_iota(jnp.int32, (nb, seq), 0)
    own = lax.broadcasted_iota(jnp.int32, (1, seq), 1) // MOBA_BLOCK
    bias_t = []
    for h in range(2):
        gh = gate[h * nb:(h + 1) * nb, :]
        cnt = jnp.zeros((nb, seq), F32)
        for b2 in range(nb):
            row = gh[b2:b2 + 1, :]
            beats = jnp.where(row > gh, 1.0, jnp.where(row == gh, jnp.where(bl > b2, 1.0, 0.0), 0.0))
            cnt = cnt + jnp.where(own > b2, beats, 0.0)
        bias_t.append(jnp.where(bl < own, jnp.where(cnt < MOBA_TOPK, 0.0, NEG), 0.0))
    pad = jnp.zeros((HEAD_DIM - nb, seq), F32)
    bias_t = jnp.concatenate([bias_t[1], pad, bias_t[0], pad], axis=0).astype(BF16)
    eye = (lax.broadcasted_iota(jnp.int32, (tq, tq), 0) == lax.broadcasted_iota(jnp.int32, (tq, tq), 1)).astype(BF16)
    for qb in range(seq // tq):
        bias_ref[qb * tq:(qb + 1) * tq, :] = lax.dot_general(eye, bias_t[:, qb * tq:(qb + 1) * tq], _ABT,
                                                             preferred_element_type=F32)

    def qblock(qi, carry):
        r0 = pl.multiple_of(qi * tq, tq)
        qs = q_ref[pl.ds(r0, tq), :] * SCALE_LOG2E
        bias = bias_ref[pl.ds(r0, tq), :]
        qm = (jnp.where(lo, qs, bias).astype(BF16), jnp.where(lo, bias, qs).astype(BF16))
        rowpos = r0 + lax.broadcasted_iota(jnp.int32, (tq, 1), 0)
        outs = _two_pass_rows(qm, qi, rowpos, (ke_ref, ko_ref), vb_ref, s_ref, m_ref, l_ref, acc_ref,
                              nkb=nb, tk=MOBA_BLOCK)
        o_ref[pl.ds(r0, tq), :] = jnp.where(lo, outs[0], outs[1])
        return carry

    lax.fori_loop(0, seq // tq, qblock, 0)


def _moba_prompt(u3):
    n, seq, _ = u3.shape
    nslab = D_MODEL // LANES
    body = functools.partial(_moba_prompt_body, seq=seq)
    return pl.pallas_call(
        body, grid=(n, nslab),
        in_specs=[
            pl.BlockSpec((None, seq, LANES), lambda b, s: (b, 0, s)),
            pl.BlockSpec((None, seq, LANES), lambda b, s: (b, 0, nslab + s)),
            pl.BlockSpec((None, seq, LANES), lambda b, s: (b, 0, 2 * nslab + s)),
        ],
        out_specs=pl.BlockSpec((None, seq, LANES), lambda b, s: (b, 0, s)),
        out_shape=jax.ShapeDtypeStruct((n, seq, D_MODEL), F32),
        scratch_shapes=_two_pass_scratch(seq, MOBA_BLOCK, MOBA_BLOCK)
        + [pltpu.VMEM((seq, LANES), BF16), pltpu.VMEM((seq, LANES), F32)],
        compiler_params=_params(2), name="moba_prompt",
    )(u3, u3, u3)


def _lambda_value(lq1_ref, lk1_ref, lq2_ref, lk2_ref, lam_init):
    a = jnp.sum(lq1_ref[...] * lk1_ref[...], axis=1, keepdims=True)
    b = jnp.sum(lq2_ref[...] * lk2_ref[...], axis=1, keepdims=True)
    return jnp.exp(a) - jnp.exp(b) + lam_init


def _diff_finish(a1, a2, lam, subln, lam_init):
    o = a1 - lam * a2
    ms = jnp.mean(o * o, axis=-1, keepdims=True)
    return (o * lax.rsqrt(ms + EPS) * subln) * (1.0 - lam_init)


def _diff_prompt_body(q_ref, k_ref, v_ref, lq1_ref, lk1_ref, lq2_ref, lk2_ref, subln_ref, o_ref,
                      kb_ref, vb_ref, s_ref, m_ref, l_ref, acc_ref, *, seq, tq, lam_init):
    lo = _lane_lo()
    kb_ref[...] = k_ref[...].astype(BF16)
    vb_ref[...] = v_ref[...].astype(BF16)
    lam = _lambda_value(lq1_ref, lk1_ref, lq2_ref, lk2_ref, lam_init)
    subln = subln_ref[...]

    def qblock(qi, carry):
        r0 = pl.multiple_of(qi * tq, tq)
        qs = q_ref[pl.ds(r0, tq), :] * SCALE_LOG2E
        qm = (jnp.where(lo, qs, 0.0).astype(BF16), jnp.where(lo, 0.0, qs).astype(BF16))
        rowpos = r0 + lax.broadcasted_iota(jnp.int32, (tq, 1), 0)
        outs = _two_pass_rows(qm, qi, rowpos,
                              kb_ref, vb_ref, s_ref, m_ref, l_ref, acc_ref, nkb=seq // tq, tk=tq)
        o_ref[pl.ds(r0, tq), :] = _diff_finish(outs[0], outs[1], lam, subln, lam_init)
        return carry

    lax.fori_loop(0, seq // tq, qblock, 0)


def _diff_prompt(u3, lams, subln, lam_init, tq=256):
    n, seq, _ = u3.shape
    nslab = D_MODEL // LANES
    body = functools.partial(_diff_prompt_body, seq=seq, tq=tq, lam_init=lam_init)
    vec = pl.BlockSpec((1, HEAD_DIM), lambda b, s: (0, 0))
    return pl.pallas_call(
        body, grid=(n, nslab),
        in_specs=[
            pl.BlockSpec((None, seq, LANES), lambda b, s: (b, 0, s)),
            pl.BlockSpec((None, seq, LANES), lambda b, s: (b, 0, nslab + s)),
            pl.BlockSpec((None, seq, LANES), lambda b, s: (b, 0, 2 * nslab + s)),
            vec, vec, vec, vec,
            pl.BlockSpec((1, LANES), lambda b, s: (0, 0)),
        ],
        out_specs=pl.BlockSpec((None, seq, LANES), lambda b, s: (b, 0, s)),
        out_shape=jax.ShapeDtypeStruct((n, seq, D_MODEL), F32),
        scratch_shapes=_two_pass_scratch(seq, tq, tq),
        compiler_params=_params(2), name="diff_prompt",
    )(u3, u3, u3, *lams, subln)


def _dilated_prompt_body(*refs, seq, unroll):
    qkv_refs = refs[:9]
    o_ref = refs[9]
    qc_ref, kc_ref, vc_ref, ocm_ref, lcm_ref, og_ref, lg_ref = refs[10:]
    lo = _lane_lo()
    blk = 128
    nblk = seq // blk
    ri = lax.broadcasted_iota(jnp.int32, (blk, 2 * blk), 0)
    ci = lax.broadcasted_iota(jnp.int32, (blk, 2 * blk), 1)
    bias_band = jnp.where((ci >= ri) & (ci <= ri + blk), 0.0, NEG)
    bias_own = jnp.where((ci >= blk) & (ci <= ri + blk), 0.0, NEG)
    kc_ref[0:blk, :] = jnp.zeros((blk, LANES), BF16)
    vc_ref[0:blk, :] = jnp.zeros((blk, LANES), BF16)

    def attend(qb, kk, vv, bias):
        outs, lses = [], []
        for h in range(2):
            qm = jnp.where(lo, qb, 0.0) if h == 0 else jnp.where(lo, 0.0, qb)
            s = lax.dot_general(qm.astype(BF16), kk, _ABT, preferred_element_type=F32) + bias
            m = jnp.max(s, axis=1, keepdims=True)
            p = jnp.exp(s - m)
            l = jnp.sum(p, axis=1, keepdims=True)
            outs.append(jnp.dot(p.astype(BF16), vv, preferred_element_type=F32) / l)
            lses.append(m + jnp.log(l))
        return jnp.where(lo, outs[0], outs[1]), jnp.where(lo, lses[0], lses[1])

    for g, (win, dil) in enumerate(C_PATTERNS):
        q_ref, k_ref, v_ref = qkv_refs[3 * g:3 * g + 3]
        ncls = seq // dil
        bpc = ncls // blk
        for r in range(dil):
            rows = pl.ds(r, ncls, stride=dil) if dil > 1 else slice(None)
            qc_ref[r * ncls:(r + 1) * ncls, :] = q_ref[rows, :] * SCALE
            kc_ref[blk + r * ncls:blk + (r + 1) * ncls, :] = k_ref[rows, :].astype(BF16)
            vc_ref[blk + r * ncls:blk + (r + 1) * ncls, :] = v_ref[rows, :].astype(BF16)

        def blocks(it, carry):
            for u in range(unroll):
                gb = it * unroll + u
                r0 = pl.multiple_of(gb * blk, blk)
                bias = jnp.where((gb % bpc) != 0, bias_band, bias_own)
                o, lse = attend(qc_ref[pl.ds(r0, blk), :], kc_ref[pl.ds(r0, 2 * blk), :],
                                vc_ref[pl.ds(r0, 2 * blk), :], bias)
                ocm_ref[pl.ds(r0, blk), :] = o
                lcm_ref[pl.ds(r0, blk), :] = lse
            return carry

        lax.fori_loop(0, nblk // unroll, blocks, 0)
        for r in range(dil):
            rows = pl.ds(r, ncls, stride=dil) if dil > 1 else slice(None)
            src = slice(r * ncls, (r + 1) * ncls)
            og_ref[g, rows, :] = ocm_ref[src, :]
            lg_ref[g, rows, :] = lcm_ref[src, :]

    lmax = jnp.maximum(jnp.maximum(lg_ref[0], lg_ref[1]), lg_ref[2])
    w = [jnp.exp(lg_ref[g] - lmax) for g in range(3)]
    den = w[0] + w[1] + w[2]
    o_ref[...] = (w[0] * og_ref[0] + w[1] * og_ref[1] + w[2] * og_ref[2]) / den


def _dilated_prompt(u3):
    n, seq, _ = u3.shape
    nslab = D_MODEL // LANES
    in_specs = []
    for g in range(3):
        for c in range(3):
            base = (3 * g + c) * nslab
            in_specs.append(pl.BlockSpec((None, seq, LANES), lambda b, s, base=base: (b, 0, base + s)))
    body = functools.partial(_dilated_prompt_body, seq=seq, unroll=8)
    return pl.pallas_call(
        body, grid=(n, nslab),
        in_specs=in_specs,
        out_specs=pl.BlockSpec((None, seq, LANES), lambda b, s: (b, 0, s)),
        out_shape=jax.ShapeDtypeStruct((n, seq, D_MODEL), F32),
        scratch_shapes=[
            pltpu.VMEM((seq, LANES), F32), pltpu.VMEM((seq + 128, LANES), BF16), pltpu.VMEM((seq + 128, LANES), BF16),
            pltpu.VMEM((seq, LANES), F32), pltpu.VMEM((seq, LANES), F32),
            pltpu.VMEM((3, seq, LANES), F32), pltpu.VMEM((3, seq, LANES), F32),
        ],
        compiler_params=_params(2), name="dilated_prompt",
    )(*([u3] * 9))


def _kmean_body(pt_ref, *refs, pages_per_block, pps):
    k_refs, o_ref = refs[:pps], refs[pps]
    p = pl.program_id(1)
    heads, dim, page = k_refs[0].shape
    bps = pps // pages_per_block
    lane = lax.broadcasted_iota(jnp.int32, (1, o_ref.shape[-1]), 1)

    @pl.when(p == 0)
    def _():
        o_ref[...] = jnp.zeros(o_ref.shape, F32)

    for b in range(bps):
        s = k_refs[b * pages_per_block][...]
        for i in range(1, pages_per_block):
            s = s + k_refs[b * pages_per_block + i][...]
        col = jnp.sum(s.reshape(heads * dim, page), axis=1, keepdims=True) * (1.0 / (pages_per_block * page))
        o_ref[...] = jnp.where(lane == p * bps + b, col, o_ref[...])


def _moba_kmean(cache_t, layer, page_table, pps=8):
    n_dec, n_pages = page_table.shape
    _, _, heads, dim, page = cache_t.shape
    ppb = MOBA_BLOCK // page
    assert pps % ppb == 0 and n_pages % pps == 0
    pages = [pl.BlockSpec((None, None, heads, dim, page), lambda b, p, pt, i=i: (layer, pt[b, pps * p + i], 0, 0, 0))
             for i in range(pps)]
    grid_spec = pltpu.PrefetchScalarGridSpec(
        num_scalar_prefetch=1, grid=(n_dec, n_pages // pps),
        in_specs=pages,
        out_specs=pl.BlockSpec((None, heads * dim, n_pages // ppb), lambda b, p, pt: (b, 0, 0)),
    )
    return pl.pallas_call(
        functools.partial(_kmean_body, pages_per_block=ppb, pps=pps), grid_spec=grid_spec,
        out_shape=jax.ShapeDtypeStruct((n_dec, heads * dim, n_pages // ppb), F32),
        compiler_params=_params(2), name="moba_kmean",
    )(page_table, *([cache_t] * pps))


def _moba_select_body(km_ref, q_ref, o_ref):
    nbk = km_ref.shape[1]
    n_heads = o_ref.shape[0]
    row = lax.broadcasted_iota(jnp.int32, (n_heads, D_MODEL), 0)
    col = lax.broadcasted_iota(jnp.int32, (n_heads, D_MODEL), 1)
    q_blk = jnp.where(col // HEAD_DIM == row, q_ref[...], 0.0)
    gate = jnp.dot(q_blk, km_ref[...], precision=lax.Precision.HIGHEST, preferred_element_type=F32)
    blk = lax.broadcasted_iota(jnp.int32, (n_heads, nbk), 1)
    out_lane = lax.broadcasted_iota(jnp.int32, o_ref.shape, 1)
    sel = jnp.zeros(o_ref.shape, jnp.int32)
    for j in range(MOBA_TOPK):
        mx = jnp.max(gate, axis=1, keepdims=True)
        idx = jnp.min(jnp.where(gate == mx, blk, nbk), axis=1, keepdims=True)
        sel = jnp.where(out_lane == j, idx, sel)
        gate = jnp.where(blk == idx, -jnp.inf, gate)
    o_ref[...] = sel


def _moba_select(kmean_t, q3, n_heads):
    n_dec, _, nbk = kmean_t.shape
    return pl.pallas_call(
        _moba_select_body, grid=(n_dec,),
        in_specs=[
            pl.BlockSpec((None, D_MODEL, nbk), lambda b: (b, 0, 0)),
            pl.BlockSpec((None, 1, D_MODEL), lambda b: (b, 0, 0)),
        ],
        out_specs=pl.BlockSpec((None, n_heads, LANES), lambda b: (b, 0, 0)),
        out_shape=jax.ShapeDtypeStruct((n_dec, n_heads, LANES), jnp.int32),
        compiler_params=_params(1), name="moba_select",
    )(kmean_t, q3)


def _moba_sample_body(pt_ref, sel_ref, q_ref, kn_ref, vn_ref, kpool, vpool, o_ref, kbuf, vbuf, sem,
                      *, layer, n_heads, ppb):
    b = pl.program_id(0)
    n_chunks = MOBA_TOPK * ppb
    slot = b % 2

    def copies(bb, sl):
        out = []
        for h in range(n_heads):
            for c in range(n_chunks):
                pg = pt_ref[bb, ppb * sel_ref[bb, (c // ppb) * n_heads + h] + c % ppb]
                out.append(pltpu.make_async_copy(kpool.at[layer, pg, h], kbuf.at[sl, h, c], sem.at[0, sl]))
                out.append(pltpu.make_async_copy(vpool.at[layer, pg, h], vbuf.at[sl, h, c], sem.at[1, sl]))
        return out

    @pl.when(b == 0)
    def _():
        for cp in copies(b, slot):
            cp.start()

    @pl.when(b + 1 < pl.num_programs(0))
    def _():
        for cp in copies(b + 1, 1 - slot):
            cp.start()

    for cp in copies(b, slot):
        cp.wait()

    q = q_ref[...] * SCALE
    s_self = jnp.sum(q * kn_ref[...], axis=-1, keepdims=True)
    for h in range(n_heads):
        qh = jnp.broadcast_to(q[h:h + 1, :], (8, q.shape[1])).astype(BF16)
        s = jnp.concatenate([jnp.dot(qh, kbuf[slot, h, c].astype(BF16), preferred_element_type=F32)
                             for c in range(n_chunks)], axis=1)
        ss = s_self[h:h + 1, :]
        m = jnp.maximum(jnp.max(s, axis=1, keepdims=True), ss)
        p = jnp.exp(s - m)
        ps = jnp.exp(ss - m)
        den = jnp.sum(p, axis=1, keepdims=True) + ps
        pb = p.astype(BF16)
        page = kbuf.shape[-1]
        o = ps * vn_ref[h:h + 1, :]
        for c in range(n_chunks):
            o = o + lax.dot_general(pb[:, c * page:(c + 1) * page], vbuf[slot, h, c].astype(BF16), _ABT,
                                    preferred_element_type=F32)
        o_ref[h:h + 1, :] = (o / den)[0:1, :]


def _moba_sample(cache_kt, cache_vt, layer, page_table, sel, q, kn, vn):
    n_dec = page_table.shape[0]
    _, _, heads, dim, page = cache_kt.shape
    ppb = MOBA_BLOCK // page
    n_chunks = MOBA_TOPK * ppb
    row = pl.BlockSpec((None, heads, dim), lambda b, pt, idx: (b, 0, 0))
    grid_spec = pltpu.PrefetchScalarGridSpec(
        num_scalar_prefetch=2, grid=(n_dec,),
        in_specs=[row, row, row, pl.BlockSpec(memory_space=pl.ANY), pl.BlockSpec(memory_space=pl.ANY)],
        out_specs=row,
        scratch_shapes=[pltpu.VMEM((2, heads, n_chunks, dim, page), F32),
                        pltpu.VMEM((2, heads, n_chunks, dim, page), F32),
                        pltpu.SemaphoreType.DMA((2, 2))],
    )
    out = pl.pallas_call(
        functools.partial(_moba_sample_body, layer=layer, n_heads=heads, ppb=ppb), grid_spec=grid_spec,
        out_shape=jax.ShapeDtypeStruct((n_dec, heads, dim), F32),
        compiler_params=_params(1), name="moba_sample",
    )(page_table, sel, q, kn, vn, cache_kt, cache_vt)
    return out.reshape(n_dec, heads * dim)


def _diff_sample_body(pt_ref, q_ref, kn_ref, vn_ref, lq1_ref, lk1_ref, lq2_ref, lk2_ref, subln_ref, *refs,
                      n_heads, page, pps, lam_init):
    k_refs, v_refs = refs[:pps], refs[pps:2 * pps]
    o_ref, m_ref, l_ref, acc_ref = refs[2 * pps:]
    pg = pl.program_id(1)
    nrow = 2 * n_heads
    row = lax.broadcasted_iota(jnp.int32, (nrow, LANES), 0)
    lane = lax.broadcasted_iota(jnp.int32, (nrow, LANES), 1)
    rowcol = lax.broadcasted_iota(jnp.int32, (nrow, 1), 0)

    @pl.when(pg == 0)
    def _():
        m_ref[...] = jnp.full(m_ref.shape, NEG, F32)
        l_ref[...] = jnp.zeros(l_ref.shape, F32)
        acc_ref[...] = jnp.zeros(acc_ref.shape, F32)

    lhs = [jnp.where(row == 2 * h + lane // HEAD_DIM, q_ref[h:h + 1, :] * SCALE, 0.0) for h in range(n_heads)]
    lhs_bf = [x.astype(BF16) for x in lhs]
    parts = []
    for i in range(pps):
        s = None
        for h in range(n_heads):
            kh = k_refs[i][pl.ds(h, page, stride=n_heads), :].astype(BF16)
            d = lax.dot_general(lhs_bf[h], kh, _ABT, preferred_element_type=F32)
            s = d if s is None else s + d
        parts.append(s)
    s = jnp.concatenate(parts, axis=1)
    m_old = m_ref[...]
    m_new = jnp.maximum(m_old, jnp.max(s, axis=1, keepdims=True))
    alpha = jnp.exp(m_old - m_new)
    p = jnp.exp(s - m_new)
    l_ref[...] = alpha * l_ref[...] + jnp.sum(p, axis=1, keepdims=True)
    m_ref[...] = m_new
    acc = alpha * acc_ref[...]
    for i in range(pps):
        pi = p[:, i * page:(i + 1) * page]
        for h in range(n_heads):
            ph = jnp.where(rowcol // 2 == h, pi, 0.0).astype(BF16)
            vh = v_refs[i][pl.ds(h, page, stride=n_heads), :].astype(BF16)
            acc = acc + jnp.dot(ph, vh, preferred_element_type=F32)
    acc_ref[...] = acc

    @pl.when(pg == pl.num_programs(1) - 1)
    def _():
        lam = _lambda_value(lq1_ref, lk1_ref, lq2_ref, lk2_ref, lam_init)
        q_all, kn_all, vn_all = lhs[0], None, None
        for h in range(1, n_heads):
            q_all = q_all + lhs[h]
        for h in range(n_heads):
            sel = rowcol // 2 == h
            kn_h = jnp.where(sel, kn_ref[h:h + 1, :], 0.0)
            vn_h = jnp.where(sel, vn_ref[h:h + 1, :], 0.0)
            kn_all = kn_h if kn_all is None else kn_all + kn_h
            vn_all = vn_h if vn_all is None else vn_all + vn_h
        s_self = jnp.sum(q_all * kn_all, axis=1, keepdims=True)
        m1 = m_ref[...]
        m2 = jnp.maximum(m1, s_self)
        a1 = jnp.exp(m1 - m2)
        p_self = jnp.exp(s_self - m2)
        a = (a1 * acc_ref[...] + p_self * vn_all) / (a1 * l_ref[...] + p_self)
        for h in range(n_heads):
            o_ref[h:h + 1, :] = _diff_finish(a[2 * h:2 * h + 1, :], a[2 * h + 1:2 * h + 2, :], lam,
                                             subln_ref[...], lam_init)


def _diff_sample(pool_k, pool_v, page_table, us4, lams, subln, lam_init, pps=8):
    n_dec, n_pages = page_table.shape
    n_heads = us4.shape[2]
    rows = pool_k.shape[1]
    assert n_pages % pps == 0
    vec = pl.BlockSpec((1, HEAD_DIM), lambda b, p, pt: (0, 0))
    pages = [pl.BlockSpec((None, rows, LANES), lambda b, p, pt, i=i: (pt[b, pps * p + i], 0, 0)) for i in range(pps)]
    grid_spec = pltpu.PrefetchScalarGridSpec(
        num_scalar_prefetch=1, grid=(n_dec, n_pages // pps),
        in_specs=[
            pl.BlockSpec((None, None, n_heads, LANES), lambda b, p, pt: (b, 0, 0, 0)),
            pl.BlockSpec((None, None, n_heads, LANES), lambda b, p, pt: (b, 1, 0, 0)),
            pl.BlockSpec((None, None, n_heads, LANES), lambda b, p, pt: (b, 2, 0, 0)),
            vec, vec, vec, vec,
            pl.BlockSpec((1, LANES), lambda b, p, pt: (0, 0)),
        ] + pages + pages,
        out_specs=pl.BlockSpec((None, n_heads, LANES), lambda b, p, pt: (b, 0, 0)),
        scratch_shapes=[pltpu.VMEM((2 * n_heads, 1), F32), pltpu.VMEM((2 * n_heads, 1), F32),
                        pltpu.VMEM((2 * n_heads, LANES), F32)],
    )
    body = functools.partial(_diff_sample_body, n_heads=n_heads, page=rows // n_heads, pps=pps, lam_init=lam_init)
    out = pl.pallas_call(
        body, grid_spec=grid_spec,
        out_shape=jax.ShapeDtypeStruct((n_dec, n_heads, LANES), F32),
        compiler_params=_params(2), name="diff_sample",
    )(page_table, us4, us4, us4, *lams, subln, *([pool_k] * pps), *([pool_v] * pps))
    return out.reshape(n_dec, D_MODEL)


def _dilated_sample_body(qkv_ref, new_ref, c0_ref, c1_ref, c2_ref, o_ref, n0_ref, n1_ref, n2_ref, *, hps):
    b = pl.program_id(0)
    c_refs = (c0_ref, c1_ref, c2_ref)
    n_refs = (n0_ref, n1_ref, n2_ref)
    n_lane = lax.broadcasted_iota(jnp.int32, (1, new_ref.shape[-1]), 1)
    for h in range(hps):
        ms, ls, accs = [], [], []
        for g, (win, dil) in enumerate(C_PATTERNS):
            rows = c_refs[g].shape[-1]
            q = jnp.broadcast_to(qkv_ref[3 * g, h:h + 1, :] * SCALE, (8, HEAD_DIM))
            kn = qkv_ref[3 * g + 1, h:h + 1, :]
            vn = qkv_ref[3 * g + 2, h:h + 1, :]
            kt = c_refs[g][0, h]
            vt = c_refs[g][1, h]
            lane = lax.broadcasted_iota(jnp.int32, (1, rows), 1)
            sc = jnp.dot(q.astype(BF16), kt.astype(BF16), preferred_element_type=F32)
            if dil > 1:
                sc = jnp.where(lane % dil == 0, sc, NEG)
            s_self = jnp.sum(q * kn, axis=1, keepdims=True)
            m = jnp.maximum(jnp.max(sc, axis=1, keepdims=True), s_self)
            p = jnp.exp(sc - m)
            p_self = jnp.exp(s_self - m)
            ms.append(m)
            ls.append(jnp.sum(p, axis=1, keepdims=True) + p_self)
            accs.append(lax.dot_general(p.astype(BF16), vt.astype(BF16), _ABT, preferred_element_type=F32)
                        + p_self * vn)
            for kv, x in ((0, kt), (1, vt)):
                col = jnp.sum(jnp.where(n_lane == b, new_ref[g, kv, h], 0.0), axis=1, keepdims=True)
                n_refs[g][kv, h] = jnp.where(lane == rows - 1, col, pltpu.roll(x, rows - 1, 1))
        mx = jnp.maximum(jnp.maximum(ms[0], ms[1]), ms[2])
        w = [jnp.exp(ms[g] - mx) for g in range(3)]
        den = w[0] * ls[0] + w[1] * ls[1] + w[2] * ls[2]
        o = (w[0] * accs[0] + w[1] * accs[1] + w[2] * accs[2]) / den
        o_ref[h:h + 1, :] = o[0:1, :]


def _dilated_sample(qkv, new_t, caches_t, hps=4):
    n_dec = qkv.shape[0]
    heads = caches_t[0].shape[2]
    hb = heads // hps
    in_specs = [
        pl.BlockSpec((None, 9, None, hps, HEAD_DIM), lambda b, s: (b, 0, s, 0, 0)),
        pl.BlockSpec((3, 2, hps, HEAD_DIM, n_dec), lambda b, s: (0, 0, s, 0, 0)),
    ]
    cache_specs = [pl.BlockSpec((None, 2, hps, HEAD_DIM, c.shape[-1]), lambda b, s: (b, 0, s, 0, 0)) for c in caches_t]
    outs = pl.pallas_call(
        functools.partial(_dilated_sample_body, hps=hps), grid=(n_dec, hb),
        in_specs=in_specs + cache_specs,
        out_specs=[pl.BlockSpec((None, None, hps, HEAD_DIM), lambda b, s: (b, s, 0, 0))] + cache_specs,
        out_shape=[jax.ShapeDtypeStruct((n_dec, hb, hps, HEAD_DIM), F32)]
        + [jax.ShapeDtypeStruct(c.shape, F32) for c in caches_t],
        compiler_params=_params(2), name="dilated_sample",
    )(qkv, new_t, *caches_t)
    return outs[0].reshape(n_dec, heads * HEAD_DIM), outs[1:]


def _rope_tables(pos):
    half = HEAD_DIM // 2
    inv = ROPE_THETA ** (-jnp.arange(half, dtype=F32) / half)
    ang = pos.astype(F32)[:, None] * inv[None, :]
    cos, sin = jnp.cos(ang), jnp.sin(ang)
    cos = jnp.concatenate([cos, cos], axis=1)
    sin = jnp.concatenate([-sin, sin], axis=1)
    return jnp.tile(cos, (1, LANES // HEAD_DIM)), jnp.tile(sin, (1, LANES // HEAD_DIM))


def _head_gain(g):
    return jnp.tile(g.astype(F32), D_MODEL // HEAD_DIM)


def kernel(x_prompt, x_sample, cache_a_k, cache_a_v, cache_b_k, cache_b_v, cache_c_kv0, cache_c_kv1, cache_c_kv2,
           page_table, norm_a, w_in_a, qn_a, kn_a, w_out_a, norm_b, w_in_b, qn_b, kn_b, lam_q1_b, lam_k1_b,
           lam_q2_b, lam_k2_b, subln_b, w_out_b, norm_c, w_in_c, qn_c, kn_c, w_out_c):
    n, seq, _ = x_prompt.shape
    n_dec, dec_seq, _ = x_sample.shape
    assert dec_seq == 1 and seq % 512 == 0
    page = cache_a_k.shape[2]
    n_pages = page_table.shape[1]
    past = n_pages * page
    n_pool = cache_a_k.shape[1]
    depth = norm_a.shape[0] + norm_b.shape[0] + norm_c.shape[0]
    tm = 1024
    a_heads = D_MODEL // HEAD_DIM
    b_heads = D_MODEL // (2 * HEAD_DIM)
    c_caches = (cache_c_kv0, cache_c_kv1, cache_c_kv2)

    cos_p, sin_p = _rope_tables(jnp.arange(seq, dtype=jnp.int32))
    cos_s, sin_s = _rope_tables(jnp.full((n_dec,), past, jnp.int32))
    bd = (jnp.arange(256)[:, None] // HEAD_DIM == jnp.arange(256)[None, :] // HEAD_DIM).astype(BF16)
    zero_gain = jnp.zeros((D_MODEL,), F32)
    cache_a_kt = jnp.transpose(cache_a_k, (0, 1, 3, 4, 2))
    cache_a_vt = jnp.transpose(cache_a_v, (0, 1, 3, 4, 2))

    hp = x_prompt.reshape(n * seq, D_MODEL)
    hs = x_sample.reshape(n_dec, D_MODEL)
    ak_p, av_p, ak_s, av_s = [], [], [], []
    bk_p, bv_p, bk_s, bv_s = [], [], [], []
    ckv_p = [[] for _ in C_PATTERNS]
    ckv_s = [[] for _ in C_PATTERNS]

    def project_both(norm, w_in, gains, modes):
        w_bf = w_in.astype(BF16)
        gains = gains.reshape(gains.shape[0], 1, COL_TILE)
        up = _project(hp, norm, w_bf, gains, modes, cos_p, sin_p, bd, tm)
        us = _project(hs, norm, w_bf, gains, modes, cos_s, sin_s, bd, n_dec)
        return up, us

    for layer in range(depth):
        kind, j = layer % 3, layer // 3
        if kind == 0:
            gains = jnp.stack([_head_gain(qn_a[j]), _head_gain(kn_a[j]), zero_gain, zero_gain])
            modes = jnp.array([1, 1, 0, 0], jnp.int32)
            up, us = project_both(norm_a[j], w_in_a[j], gains, modes)
            op = _moba_prompt(up.reshape(n, seq, 4 * D_MODEL)).reshape(n * seq, D_MODEL)
            us3 = us.reshape(n_dec, 4, a_heads, HEAD_DIM)
            kmean = _moba_kmean(cache_a_kt, j, page_table)
            sel = _moba_select(kmean, us.reshape(n_dec, 1, 4 * D_MODEL), a_heads)
            sel = jnp.swapaxes(sel[:, :, :MOBA_TOPK], 1, 2).reshape(n_dec, MOBA_TOPK * a_heads)
            os_ = _moba_sample(cache_a_kt, cache_a_vt, j, page_table, sel, us3[:, 0], us3[:, 1], us3[:, 2])
            hp = _out_project(op, up, hp, w_out_a[j].astype(BF16), tm)
            hs = _out_project(os_, us, hs, w_out_a[j].astype(BF16), n_dec)
            ak_p.append(up[:, D_MODEL:2 * D_MODEL].reshape(n, seq, a_heads, HEAD_DIM))
            av_p.append(up[:, 2 * D_MODEL:3 * D_MODEL].reshape(n, seq, a_heads, HEAD_DIM))
            ak_s.append(us[:, D_MODEL:2 * D_MODEL].reshape(n_dec, 1, a_heads, HEAD_DIM))
            av_s.append(us[:, 2 * D_MODEL:3 * D_MODEL].reshape(n_dec, 1, a_heads, HEAD_DIM))
        elif kind == 1:
            lam_init = 0.8 - 0.6 * math.exp(-0.3 * layer)
            gains = jnp.stack([_head_gain(qn_b[j]), _head_gain(kn_b[j]), zero_gain, zero_gain])
            modes = jnp.array([1, 1, 0, 0], jnp.int32)
            up, us = project_both(norm_b[j], w_in_b[j], gains, modes)
            lams = tuple(v[j].astype(F32).reshape(1, HEAD_DIM) for v in (lam_q1_b, lam_k1_b, lam_q2_b, lam_k2_b))
            subln = subln_b[j].astype(F32).reshape(1, LANES)
            op = _diff_prompt(up.reshape(n, seq, 4 * D_MODEL), lams, subln, lam_init).reshape(n * seq, D_MODEL)
            pool_k = cache_b_k[j].reshape(n_pool, page * b_heads, LANES)
            pool_v = cache_b_v[j].reshape(n_pool, page * b_heads, LANES)
            os_ = _diff_sample(pool_k, pool_v, page_table, us.reshape(n_dec, 4, b_heads, LANES), lams, subln, lam_init)
            hp = _out_project(op, up, hp, w_out_b[j].astype(BF16), tm)
            hs = _out_project(os_, us, hs, w_out_b[j].astype(BF16), n_dec)
            bk_p.append(up[:, D_MODEL:2 * D_MODEL].reshape(n, seq, b_heads, LANES))
            bv_p.append(up[:, 2 * D_MODEL:3 * D_MODEL].reshape(n, seq, b_heads, LANES))
            bk_s.append(us[:, D_MODEL:2 * D_MODEL].reshape(n_dec, 1, b_heads, LANES))
            bv_s.append(us[:, 2 * D_MODEL:3 * D_MODEL].reshape(n_dec, 1, b_heads, LANES))
        else:
            gain_rows, mode_vals = [], []
            for g in range(len(C_PATTERNS)):
                gain_rows += [_head_gain(qn_c[j, g]), _head_gain(kn_c[j, g]), zero_gain]
                mode_vals += [1, 1, 0]
            gains = jnp.stack(gain_rows + [zero_gain])
            modes = jnp.array(mode_vals + [0], jnp.int32)
            up, us = project_both(norm_c[j], w_in_c[j], gains, modes)
            e = up.shape[1]
            op = _dilated_prompt(up.reshape(n, seq, e)).reshape(n * seq, D_MODEL)
            n_grp = len(C_PATTERNS)
            hps = 4
            qkv = us[:, :3 * n_grp * D_MODEL].reshape(n_dec, 3 * n_grp, a_heads // hps, hps, HEAD_DIM)
            new_kv = us[:, :3 * n_grp * D_MODEL].reshape(n_dec, n_grp, 3, a_heads, HEAD_DIM)[:, :, 1:]
            new_t = jnp.transpose(new_kv, (1, 2, 3, 4, 0))
            caches_t = []
            for g, (win, dil) in enumerate(C_PATTERNS):
                assert c_caches[g].shape[2] == win and win // dil == 128
                caches_t.append(jnp.transpose(c_caches[g][j], (0, 2, 3, 4, 1)))
            os_, adv = _dilated_sample(qkv, new_t, caches_t, hps)
            hp = _out_project(op, up, hp, w_out_c[j].astype(BF16), tm)
            hs = _out_project(os_, us, hs, w_out_c[j].astype(BF16), n_dec)
            up3 = up.reshape(n, seq, e)
            for g, (win, dil) in enumerate(C_PATTERNS):
                c0 = (3 * g + 1) * D_MODEL
                keep_p = min(win, seq)
                ckv_p[g].append(up3[:, seq - keep_p:, c0:c0 + 2 * D_MODEL].reshape(n, keep_p, 2, a_heads, HEAD_DIM))
                ckv_s[g].append(jnp.transpose(adv[g], (0, 4, 1, 2, 3)))

    return (hp.reshape(n, seq, D_MODEL), hs.reshape(n_dec, 1, D_MODEL),
            jnp.stack(ak_p), jnp.stack(av_p), jnp.stack(ak_s), jnp.stack(av_s),
            jnp.stack(bk_p), jnp.stack(bv_p), jnp.stack(bk_s), jnp.stack(bv_s),
            jnp.stack(ckv_p[0]), jnp.stack(ckv_p[1]), jnp.stack(ckv_p[2]),
            jnp.stack(ckv_s[0]), jnp.stack(ckv_s[1]), jnp.stack(ckv_s[2]))
```

```python
import functools
import math

import jax
import jax.numpy as jnp
from jax import lax
from jax.experimental import pallas as pl
from jax.experimental.pallas import tpu as pltpu

F32 = jnp.float32
BF16 = jnp.bfloat16

D_MODEL = 1024
HEAD_DIM = 64
LANES = 128
COL_TILE = 1024
EPS = 1e-6
ROPE_THETA = 10000.0
SCALE = HEAD_DIM ** -0.5
SCALE_LOG2E = SCALE * math.log2(math.e)
NEG = -1e30
MOBA_BLOCK = 256
MOBA_TOPK = 3
C_PATTERNS = ((128, 1), (512, 4), (2048, 16))
VMEM_LIMIT = 48 * 1024 * 1024

_ABT = (((1,), (1,)), ((), ()))


def _params(n_axes):
    return pltpu.CompilerParams(dimension_semantics=("arbitrary",) * n_axes,
                                vmem_limit_bytes=VMEM_LIMIT)


def _lane_lo():
    return lax.broadcasted_iota(jnp.int32, (1, LANES), 1) < HEAD_DIM


def _row_picks_head():
    row = lax.broadcasted_iota(jnp.int32, (8, LANES), 0)
    lane = lax.broadcasted_iota(jnp.int32, (8, LANES), 1)
    return (lane // HEAD_DIM) == row


def _proj_body(mode_ref, x_ref, g_ref, w_ref, gain_ref, cos_ref, sin_ref, bd_ref, o_ref, xn_ref):
    j = pl.program_id(1)

    @pl.when(j == 0)
    def _():
        x = x_ref[...]
        ms = jnp.mean(x * x, axis=-1, keepdims=True)
        xn_ref[...] = (x * lax.rsqrt(ms + EPS) * g_ref[...]).astype(BF16)

    u = jnp.dot(xn_ref[...], w_ref[...], preferred_element_type=F32)
    mode = mode_ref[j]

    @pl.when(mode == 0)
    def _():
        o_ref[...] = u

    @pl.when(mode == 1)
    def _():
        gain = gain_ref[...]
        cos = cos_ref[...]
        sin = sin_ref[...]
        lane = lax.broadcasted_iota(jnp.int32, (1, LANES), 1)
        first_half = (lane % HEAD_DIM) < (HEAD_DIM // 2)
        for c in range(COL_TILE // 256):
            uc = u[:, c * 256:(c + 1) * 256]
            ss = jnp.dot((uc * uc).astype(BF16), bd_ref[...], preferred_element_type=F32)
            y = uc * lax.rsqrt(ss * (1.0 / HEAD_DIM) + EPS) * gain[:, c * 256:(c + 1) * 256]
            for s in range(2):
                ys = y[:, s * LANES:(s + 1) * LANES]
                rot = jnp.where(first_half, pltpu.roll(ys, 96, 1), pltpu.roll(ys, 32, 1))
                o_ref[:, c * 256 + s * LANES:c * 256 + (s + 1) * LANES] = ys * cos + rot * sin


def _project(x, g, w_bf, gains, modes, cos, sin, bd, tm):
    t, e = x.shape[0], w_bf.shape[1]
    nt = e // COL_TILE
    npb = cos.shape[0] // tm
    grid_spec = pltpu.PrefetchScalarGridSpec(
        num_scalar_prefetch=1,
        grid=(t // tm, nt),
        in_specs=[
            pl.BlockSpec((tm, D_MODEL), lambda i, j, m: (i, 0)),
            pl.BlockSpec((1, D_MODEL), lambda i, j, m: (0, 0)),
            pl.BlockSpec((D_MODEL, COL_TILE), lambda i, j, m: (0, j)),
            pl.BlockSpec((None, 1, COL_TILE), lambda i, j, m: (j, 0, 0)),
            pl.BlockSpec((tm, LANES), lambda i, j, m: (i % npb, 0)),
            pl.BlockSpec((tm, LANES), lambda i, j, m: (i % npb, 0)),
            pl.BlockSpec((256, 256), lambda i, j, m: (0, 0)),
        ],
        out_specs=pl.BlockSpec((tm, COL_TILE), lambda i, j, m: (i, j)),
        scratch_shapes=[pltpu.VMEM((tm, D_MODEL), BF16)],
    )
    return pl.pallas_call(
        _proj_body, grid_spec=grid_spec,
        out_shape=jax.ShapeDtypeStruct((t, e), F32),
        compiler_params=_params(2), name="proj_in",
    )(modes, x, g.reshape(1, D_MODEL), w_bf, gains, cos, sin, bd)


def _outproj_body(o_ref, gate_ref, h_ref, w_ref, y_ref):
    g = gate_ref[...]
    z = o_ref[...] * (g * jax.nn.sigmoid(g))
    y_ref[...] = h_ref[...] + jnp.dot(z.astype(BF16), w_ref[...], preferred_element_type=F32)


def _out_project(o, u, h, w_bf, tm):
    t = o.shape[0]
    gate_blk = u.shape[1] // COL_TILE - 1
    return pl.pallas_call(
        _outproj_body, grid=(t // tm,),
        in_specs=[
            pl.BlockSpec((tm, D_MODEL), lambda i: (i, 0)),
            pl.BlockSpec((tm, COL_TILE), lambda i: (i, gate_blk)),
            pl.BlockSpec((tm, D_MODEL), lambda i: (i, 0)),
            pl.BlockSpec((D_MODEL, D_MODEL), lambda i: (0, 0)),
        ],
        out_specs=pl.BlockSpec((tm, D_MODEL), lambda i: (i, 0)),
        out_shape=jax.ShapeDtypeStruct((t, D_MODEL), F32),
        compiler_params=_params(1), name="proj_out",
    )(o, u, h, w_bf)


def _two_pass_rows(qm, qi, rowpos, kb_ref, vb_ref, s_ref, m_ref, l_ref, acc_ref, *, nq, tq, tk):
    half = tk // 2
    per = tq // tk

    def score_tiles(n_past, n_all):
        for h in range(2):
            tmax = None
            k_ref = kb_ref[h] if isinstance(kb_ref, (tuple, list)) else kb_ref
            for kb in range(n_all):
                s = lax.dot_general(qm[h], k_ref[kb * tk:(kb + 1) * tk, :], _ABT, preferred_element_type=F32)
                if kb >= n_past:
                    colpos = kb * tk + lax.broadcasted_iota(jnp.int32, (1, tk), 1)
                    s = jnp.where(colpos <= rowpos, s, NEG)
                s_ref[h, kb] = s
                t = jnp.maximum(s[:, :half], s[:, half:])
                tmax = t if tmax is None else jnp.maximum(tmax, t)
            m_ref[h] = jnp.max(tmax, axis=1, keepdims=True)

    for v in range(nq):
        pl.when(qi == v)(functools.partial(score_tiles, v * per, (v + 1) * per))

    m = [m_ref[h] for h in range(2)]

    def value_tiles(n_all):
        for h in range(2):
            psum, acc = None, None
            for kb in range(n_all):
                p = jnp.exp2(s_ref[h, kb] - m[h])
                t = p[:, :half] + p[:, half:]
                psum = t if psum is None else psum + t
                d = jnp.dot(p.astype(BF16), vb_ref[kb * tk:(kb + 1) * tk, :], preferred_element_type=F32)
                acc = d if acc is None else acc + d
            l_ref[h] = jnp.sum(psum, axis=1, keepdims=True)
            acc_ref[h] = acc

    for v in range(nq):
        pl.when(qi == v)(functools.partial(value_tiles, (v + 1) * per))

    return [acc_ref[h] / l_ref[h] for h in range(2)]


def _two_pass_scratch(seq, tq, tk):
    return [
        pltpu.VMEM((seq, LANES), BF16), pltpu.VMEM((seq, LANES), BF16),
        pltpu.VMEM((2, seq // tk, tq, tk), F32),
        pltpu.VMEM((2, tq, 1), F32), pltpu.VMEM((2, tq, 1), F32), pltpu.VMEM((2, tq, LANES), F32),
    ]


def _moba_prompt_body(q_ref, k_ref, v_ref, o_ref, ke_ref, vb_ref, s_ref, m_ref, l_ref, acc_ref, ko_ref, bias_ref, *, seq, tq):
    tb = MOBA_BLOCK
    nb = seq // MOBA_BLOCK
    lo = _lane_lo()
    lane = lax.broadcasted_iota(jnp.int32, (1, LANES), 1)
    k = k_ref[...]
    kblock = lax.broadcasted_iota(jnp.int32, (seq, 1), 0) // MOBA_BLOCK
    ke_ref[...] = jnp.where(lo, k, jnp.where(lane - HEAD_DIM == kblock, 1.0, 0.0)).astype(BF16)
    ko_ref[...] = jnp.where(lo, jnp.where(lane == kblock, 1.0, 0.0), k).astype(BF16)
    vb_ref[...] = v_ref[...].astype(BF16)
    kmean = jnp.concatenate(
        [jnp.mean(k_ref[b * MOBA_BLOCK:(b + 1) * MOBA_BLOCK, :], axis=0, keepdims=True) for b in range(nb)],
        axis=0)
    kcat = jnp.concatenate([jnp.where(lo, kmean, 0.0), jnp.where(lo, 0.0, kmean)], axis=0)

    gate = lax.dot_general(kcat, q_ref[...], _ABT, precision=lax.Precision.HIGHEST, preferred_element_type=F32)
    bl = lax.broadcasted_iota(jnp.int32, (nb, seq), 0)
    own = lax.broadcasted_iota(jnp.int32, (1, seq), 1) // MOBA_BLOCK
    bias_t = []
    for h in range(2):
        gh = gate[h * nb:(h + 1) * nb, :]
        cnt = jnp.zeros((nb, seq), F32)
        for b2 in range(nb):
            row = gh[b2:b2 + 1, :]
            beats = jnp.where(row > gh, 1.0, jnp.where(row == gh, jnp.where(bl > b2, 1.0, 0.0), 0.0))
            cnt = cnt + jnp.where(own > b2, beats, 0.0)
        bias_t.append(jnp.where(bl < own, jnp.where(cnt < MOBA_TOPK, 0.0, NEG), 0.0))
    pad = jnp.zeros((HEAD_DIM - nb, seq), F32)
    bias_t = jnp.concatenate([bias_t[1], pad, bias_t[0], pad], axis=0).astype(BF16)
    eye = (lax.broadcasted_iota(jnp.int32, (tb, tb), 0) == lax.broadcasted_iota(jnp.int32, (tb, tb), 1)).astype(BF16)
    for qb in range(seq // tb):
        bias_ref[qb * tb:(qb + 1) * tb, :] = lax.dot_general(eye, bias_t[:, qb * tb:(qb + 1) * tb], _ABT,
                                                             preferred_element_type=F32)

    def qblock(qi, carry):
        r0 = pl.multiple_of(qi * tq, tq)
        qs = q_ref[pl.ds(r0, tq), :] * SCALE_LOG2E
        bias = bias_ref[pl.ds(r0, tq), :]
        qm = (jnp.where(lo, qs, bias).astype(BF16), jnp.where(lo, bias, qs).astype(BF16))
        rowpos = r0 + lax.broadcasted_iota(jnp.int32, (tq, 1), 0)
        outs = _two_pass_rows(qm, qi, rowpos, (ke_ref, ko_ref), vb_ref, s_ref, m_ref, l_ref, acc_ref,
                              nq=seq // tq, tq=tq, tk=MOBA_BLOCK)
        o_ref[pl.ds(r0, tq), :] = jnp.where(lo, outs[0], outs[1])
        return carry

    lax.fori_loop(0, seq // tq, qblock, 0)


def _moba_prompt(u3, tq=512):
    n, seq, _ = u3.shape
    nslab = D_MODEL // LANES
    body = functools.partial(_moba_prompt_body, seq=seq, tq=tq)
    return pl.pallas_call(
        body, grid=(n, nslab),
        in_specs=[
            pl.BlockSpec((None, seq, LANES), lambda b, s: (b, 0, s)),
            pl.BlockSpec((None, seq, LANES), lambda b, s: (b, 0, nslab + s)),
            pl.BlockSpec((None, seq, LANES), lambda b, s: (b, 0, 2 * nslab + s)),
        ],
        out_specs=pl.BlockSpec((None, seq, LANES), lambda b, s: (b, 0, s)),
        out_shape=jax.ShapeDtypeStruct((n, seq, D_MODEL), F32),
        scratch_shapes=_two_pass_scratch(seq, tq, MOBA_BLOCK)
        + [pltpu.VMEM((seq, LANES), BF16), pltpu.VMEM((seq, LANES), F32)],
        compiler_params=_params(2), name="moba_prompt",
    )(u3, u3, u3)


def _lambda_value(lq1_ref, lk1_ref, lq2_ref, lk2_ref, lam_init):
    a = jnp.sum(lq1_ref[...] * lk1_ref[...], axis=1, keepdims=True)
    b = jnp.sum(lq2_ref[...] * lk2_ref[...], axis=1, keepdims=True)
    return jnp.exp(a) - jnp.exp(b) + lam_init


def _diff_finish(a1, a2, lam, subln, lam_init):
    o = a1 - lam * a2
    ms = jnp.mean(o * o, axis=-1, keepdims=True)
    return (o * lax.rsqrt(ms + EPS) * subln) * (1.0 - lam_init)


def _diff_prompt_body(q_ref, k_ref, v_ref, lq1_ref, lk1_ref, lq2_ref, lk2_ref, subln_ref, o_ref,
                      kb_ref, vb_ref, s_ref, m_ref, l_ref, acc_ref, *, seq, tq, tk, lam_init):
    lo = _lane_lo()
    kb_ref[...] = k_ref[...].astype(BF16)
    vb_ref[...] = v_ref[...].astype(BF16)
    lam = _lambda_value(lq1_ref, lk1_ref, lq2_ref, lk2_ref, lam_init)
    subln = subln_ref[...]

    def qblock(qi, carry):
        r0 = pl.multiple_of(qi * tq, tq)
        qs = q_ref[pl.ds(r0, tq), :] * SCALE_LOG2E
        qm = (jnp.where(lo, qs, 0.0).astype(BF16), jnp.where(lo, 0.0, qs).astype(BF16))
        rowpos = r0 + lax.broadcasted_iota(jnp.int32, (tq, 1), 0)
        outs = _two_pass_rows(qm, qi, rowpos,
                              kb_ref, vb_ref, s_ref, m_ref, l_ref, acc_ref, nq=seq // tq, tq=tq, tk=tk)
        o_ref[pl.ds(r0, tq), :] = _diff_finish(outs[0], outs[1], lam, subln, lam_init)
        return carry

    lax.fori_loop(0, seq // tq, qblock, 0)


def _diff_prompt(u3, lams, subln, lam_init, tq=512, tk=256):
    n, seq, _ = u3.shape
    nslab = D_MODEL // LANES
    body = functools.partial(_diff_prompt_body, seq=seq, tq=tq, tk=tk, lam_init=lam_init)
    vec = pl.BlockSpec((1, HEAD_DIM), lambda b, s: (0, 0))
    return pl.pallas_call(
        body, grid=(n, nslab),
        in_specs=[
            pl.BlockSpec((None, seq, LANES), lambda b, s: (b, 0, s)),
            pl.BlockSpec((None, seq, LANES), lambda b, s: (b, 0, nslab + s)),
            pl.BlockSpec((None, seq, LANES), lambda b, s: (b, 0, 2 * nslab + s)),
            vec, vec, vec, vec,
            pl.BlockSpec((1, LANES), lambda b, s: (0, 0)),
        ],
        out_specs=pl.BlockSpec((None, seq, LANES), lambda b, s: (b, 0, s)),
        out_shape=jax.ShapeDtypeStruct((n, seq, D_MODEL), F32),
        scratch_shapes=_two_pass_scratch(seq, tq, tk),
        compiler_params=_params(2), name="diff_prompt",
    )(u3, u3, u3, *lams, subln)


def _dilated_prompt_body(*refs, seq, unroll):
    qkv_refs = refs[:9]
    o_ref = refs[9]
    qc_ref, kc_ref, vc_ref, ocm_ref, lcm_ref, og_ref, lg_ref = refs[10:]
    lo = _lane_lo()
    blk = 128
    nblk = seq // blk
    ri = lax.broadcasted_iota(jnp.int32, (blk, 2 * blk), 0)
    ci = lax.broadcasted_iota(jnp.int32, (blk, 2 * blk), 1)
    bias_band = jnp.where((ci >= ri) & (ci <= ri + blk), 0.0, NEG)
    bias_own = jnp.where((ci >= blk) & (ci <= ri + blk), 0.0, NEG)
    kc_ref[0:blk, :] = jnp.zeros((blk, LANES), BF16)
    vc_ref[0:blk, :] = jnp.zeros((blk, LANES), BF16)

    def attend(qb, kk, vv, bias):
        outs, lses = [], []
        for h in range(2):
            qm = jnp.where(lo, qb, 0.0) if h == 0 else jnp.where(lo, 0.0, qb)
            s = lax.dot_general(qm.astype(BF16), kk, _ABT, preferred_element_type=F32) + bias
            m = jnp.max(s, axis=1, keepdims=True)
            p = jnp.exp(s - m)
            l = jnp.sum(p, axis=1, keepdims=True)
            outs.append(jnp.dot(p.astype(BF16), vv, preferred_element_type=F32) / l)
            lses.append(m + jnp.log(l))
        return jnp.where(lo, outs[0], outs[1]), jnp.where(lo, lses[0], lses[1])

    for g, (win, dil) in enumerate(C_PATTERNS):
        q_ref, k_ref, v_ref = qkv_refs[3 * g:3 * g + 3]
        ncls = seq // dil
        bpc = ncls // blk
        for r in range(dil):
            rows = pl.ds(r, ncls, stride=dil) if dil > 1 else slice(None)
            qc_ref[r * ncls:(r + 1) * ncls, :] = q_ref[rows, :] * SCALE
            kc_ref[blk + r * ncls:blk + (r + 1) * ncls, :] = k_ref[rows, :].astype(BF16)
            vc_ref[blk + r * ncls:blk + (r + 1) * ncls, :] = v_ref[rows, :].astype(BF16)

        def blocks(it, carry):
            for u in range(unroll):
                gb = it * unroll + u
                r0 = pl.multiple_of(gb * blk, blk)
                bias = jnp.where((gb % bpc) != 0, bias_band, bias_own)
                o, lse = attend(qc_ref[pl.ds(r0, blk), :], kc_ref[pl.ds(r0, 2 * blk), :],
                                vc_ref[pl.ds(r0, 2 * blk), :], bias)
                ocm_ref[pl.ds(r0, blk), :] = o
                lcm_ref[pl.ds(r0, blk), :] = lse
            return carry

        lax.fori_loop(0, nblk // unroll, blocks, 0)
        for r in range(dil):
            rows = pl.ds(r, ncls, stride=dil) if dil > 1 else slice(None)
            src = slice(r * ncls, (r + 1) * ncls)
            og_ref[g, rows, :] = ocm_ref[src, :]
            lg_ref[g, rows, :] = lcm_ref[src, :]

    lmax = jnp.maximum(jnp.maximum(lg_ref[0], lg_ref[1]), lg_ref[2])
    w = [jnp.exp(lg_ref[g] - lmax) for g in range(3)]
    den = w[0] + w[1] + w[2]
    o_ref[...] = (w[0] * og_ref[0] + w[1] * og_ref[1] + w[2] * og_ref[2]) / den


def _dilated_prompt(u3):
    n, seq, _ = u3.shape
    nslab = D_MODEL // LANES
    in_specs = []
    for g in range(3):
        for c in range(3):
            base = (3 * g + c) * nslab
            in_specs.append(pl.BlockSpec((None, seq, LANES), lambda b, s, base=base: (b, 0, base + s)))
    body = functools.partial(_dilated_prompt_body, seq=seq, unroll=8)
    return pl.pallas_call(
        body, grid=(n, nslab),
        in_specs=in_specs,
        out_specs=pl.BlockSpec((None, seq, LANES), lambda b, s: (b, 0, s)),
        out_shape=jax.ShapeDtypeStruct((n, seq, D_MODEL), F32),
        scratch_shapes=[
            pltpu.VMEM((seq, LANES), F32), pltpu.VMEM((seq + 128, LANES), BF16), pltpu.VMEM((seq + 128, LANES), BF16),
            pltpu.VMEM((seq, LANES), F32), pltpu.VMEM((seq, LANES), F32),
            pltpu.VMEM((3, seq, LANES), F32), pltpu.VMEM((3, seq, LANES), F32),
        ],
        compiler_params=_params(2), name="dilated_prompt",
    )(*([u3] * 9))


def _kmean_body(pt_ref, *refs, pages_per_block, pps):
    k_refs, o_ref = refs[:pps], refs[pps]
    p = pl.program_id(1)
    heads, dim, page = k_refs[0].shape
    bps = pps // pages_per_block
    lane = lax.broadcasted_iota(jnp.int32, (1, o_ref.shape[-1]), 1)

    @pl.when(p == 0)
    def _():
        o_ref[...] = jnp.zeros(o_ref.shape, F32)

    for b in range(bps):
        s = k_refs[b * pages_per_block][...]
        for i in range(1, pages_per_block):
            s = s + k_refs[b * pages_per_block + i][...]
        col = jnp.sum(s.reshape(heads * dim, page), axis=1, keepdims=True) * (1.0 / (pages_per_block * page))
        o_ref[...] = jnp.where(lane == p * bps + b, col, o_ref[...])


def _moba_kmean(cache_t, layer, page_table, pps=8):
    n_dec, n_pages = page_table.shape
    _, _, heads, dim, page = cache_t.shape
    ppb = MOBA_BLOCK // page
    assert pps % ppb == 0 and n_pages % pps == 0
    pages = [pl.BlockSpec((None, None, heads, dim, page), lambda b, p, pt, i=i: (layer, pt[b, pps * p + i], 0, 0, 0))
             for i in range(pps)]
    grid_spec = pltpu.PrefetchScalarGridSpec(
        num_scalar_prefetch=1, grid=(n_dec, n_pages // pps),
        in_specs=pages,
        out_specs=pl.BlockSpec((None, heads * dim, n_pages // ppb), lambda b, p, pt: (b, 0, 0)),
    )
    return pl.pallas_call(
        functools.partial(_kmean_body, pages_per_block=ppb, pps=pps), grid_spec=grid_spec,
        out_shape=jax.ShapeDtypeStruct((n_dec, heads * dim, n_pages // ppb), F32),
        compiler_params=_params(2), name="moba_kmean",
    )(page_table, *([cache_t] * pps))


def _moba_select_body(km_ref, q_ref, o_ref):
    nbk = km_ref.shape[1]
    n_heads = o_ref.shape[0]
    row = lax.broadcasted_iota(jnp.int32, (n_heads, D_MODEL), 0)
    col = lax.broadcasted_iota(jnp.int32, (n_heads, D_MODEL), 1)
    q_blk = jnp.where(col // HEAD_DIM == row, q_ref[...], 0.0)
    gate = jnp.dot(q_blk, km_ref[...], precision=lax.Precision.HIGHEST, preferred_element_type=F32)
    blk = lax.broadcasted_iota(jnp.int32, (n_heads, nbk), 1)
    out_lane = lax.broadcasted_iota(jnp.int32, o_ref.shape, 1)
    sel = jnp.zeros(o_ref.shape, jnp.int32)
    for j in range(MOBA_TOPK):
        mx = jnp.max(gate, axis=1, keepdims=True)
        idx = jnp.min(jnp.where(gate == mx, blk, nbk), axis=1, keepdims=True)
        sel = jnp.where(out_lane == j, idx, sel)
        gate = jnp.where(blk == idx, -jnp.inf, gate)
    o_ref[...] = sel


def _moba_select(kmean_t, q3, n_heads):
    n_dec, _, nbk = kmean_t.shape
    return pl.pallas_call(
        _moba_select_body, grid=(n_dec,),
        in_specs=[
            pl.BlockSpec((None, D_MODEL, nbk), lambda b: (b, 0, 0)),
            pl.BlockSpec((None, 1, D_MODEL), lambda b: (b, 0, 0)),
        ],
        out_specs=pl.BlockSpec((None, n_heads, LANES), lambda b: (b, 0, 0)),
        out_shape=jax.ShapeDtypeStruct((n_dec, n_heads, LANES), jnp.int32),
        compiler_params=_params(1), name="moba_select",
    )(kmean_t, q3)


def _moba_sample_body(pt_ref, sel_ref, q_ref, kn_ref, vn_ref, kpool, vpool, o_ref, kbuf, vbuf, sem,
                      *, layer, n_heads, ppb):
    b = pl.program_id(0)
    n_chunks = MOBA_TOPK * ppb
    slot = b % 2

    def copies(bb, sl):
        out = []
        for h in range(n_heads):
            for c in range(n_chunks):
                pg = pt_ref[bb, ppb * sel_ref[bb, (c // ppb) * n_heads + h] + c % ppb]
                out.append(pltpu.make_async_copy(kpool.at[layer, pg, h], kbuf.at[sl, h, c], sem.at[0, sl]))
                out.append(pltpu.make_async_copy(vpool.at[layer, pg, h], vbuf.at[sl, h, c], sem.at[1, sl]))
        return out

    @pl.when(b == 0)
    def _():
        for cp in copies(b, slot):
            cp.start()

    @pl.when(b + 1 < pl.num_programs(0))
    def _():
        for cp in copies(b + 1, 1 - slot):
            cp.start()

    for cp in copies(b, slot):
        cp.wait()

    q = q_ref[...] * SCALE
    s_self = jnp.sum(q * kn_ref[...], axis=-1, keepdims=True)
    for h in range(n_heads):
        qh = jnp.broadcast_to(q[h:h + 1, :], (8, q.shape[1])).astype(BF16)
        s = jnp.concatenate([jnp.dot(qh, kbuf[slot, h, c].astype(BF16), preferred_element_type=F32)
                             for c in range(n_chunks)], axis=1)
        ss = s_self[h:h + 1, :]
        m = jnp.maximum(jnp.max(s, axis=1, keepdims=True), ss)
        p = jnp.exp(s - m)
        ps = jnp.exp(ss - m)
        den = jnp.sum(p, axis=1, keepdims=True) + ps
        pb = p.astype(BF16)
        page = kbuf.shape[-1]
        o = ps * vn_ref[h:h + 1, :]
        for c in range(n_chunks):
            o = o + lax.dot_general(pb[:, c * page:(c + 1) * page], vbuf[slot, h, c].astype(BF16), _ABT,
                                    preferred_element_type=F32)
        o_ref[h:h + 1, :] = (o / den)[0:1, :]


def _moba_sample(cache_kt, cache_vt, layer, page_table, sel, q, kn, vn):
    n_dec = page_table.shape[0]
    _, _, heads, dim, page = cache_kt.shape
    ppb = MOBA_BLOCK // page
    n_chunks = MOBA_TOPK * ppb
    row = pl.BlockSpec((None, heads, dim), lambda b, pt, idx: (b, 0, 0))
    grid_spec = pltpu.PrefetchScalarGridSpec(
        num_scalar_prefetch=2, grid=(n_dec,),
        in_specs=[row, row, row, pl.BlockSpec(memory_space=pl.ANY), pl.BlockSpec(memory_space=pl.ANY)],
        out_specs=row,
        scratch_shapes=[pltpu.VMEM((2, heads, n_chunks, dim, page), F32),
                        pltpu.VMEM((2, heads, n_chunks, dim, page), F32),
                        pltpu.SemaphoreType.DMA((2, 2))],
    )
    out = pl.pallas_call(
        functools.partial(_moba_sample_body, layer=layer, n_heads=heads, ppb=ppb), grid_spec=grid_spec,
        out_shape=jax.ShapeDtypeStruct((n_dec, heads, dim), F32),
        compiler_params=_params(1), name="moba_sample",
    )(page_table, sel, q, kn, vn, cache_kt, cache_vt)
    return out.reshape(n_dec, heads * dim)


def _diff_sample_body(pt_ref, q_ref, kn_ref, vn_ref, lq1_ref, lk1_ref, lq2_ref, lk2_ref, subln_ref, *refs,
                      n_heads, page, pps, lam_init):
    k_refs, v_refs = refs[:pps], refs[pps:2 * pps]
    o_ref, m_ref, l_ref, acc_ref = refs[2 * pps:]
    pg = pl.program_id(1)
    nrow = 2 * n_heads
    row = lax.broadcasted_iota(jnp.int32, (nrow, LANES), 0)
    lane = lax.broadcasted_iota(jnp.int32, (nrow, LANES), 1)
    rowcol = lax.broadcasted_iota(jnp.int32, (nrow, 1), 0)

    @pl.when(pg == 0)
    def _():
        m_ref[...] = jnp.full(m_ref.shape, NEG, F32)
        l_ref[...] = jnp.zeros(l_ref.shape, F32)
        acc_ref[...] = jnp.zeros(acc_ref.shape, F32)

    lhs = [jnp.where(row == 2 * h + lane // HEAD_DIM, q_ref[h:h + 1, :] * SCALE, 0.0) for h in range(n_heads)]
    lhs_bf = [x.astype(BF16) for x in lhs]
    parts = []
    for i in range(pps):
        s = None
        for h in range(n_heads):
            kh = k_refs[i][pl.ds(h, page, stride=n_heads), :].astype(BF16)
            d = lax.dot_general(lhs_bf[h], kh, _ABT, preferred_element_type=F32)
            s = d if s is None else s + d
        parts.append(s)
    s = jnp.concatenate(parts, axis=1)
    m_old = m_ref[...]
    m_new = jnp.maximum(m_old, jnp.max(s, axis=1, keepdims=True))
    alpha = jnp.exp(m_old - m_new)
    p = jnp.exp(s - m_new)
    l_ref[...] = alpha * l_ref[...] + jnp.sum(p, axis=1, keepdims=True)
    m_ref[...] = m_new
    acc = alpha * acc_ref[...]
    for i in range(pps):
        pi = p[:, i * page:(i + 1) * page]
        for h in range(n_heads):
            ph = jnp.where(rowcol // 2 == h, pi, 0.0).astype(BF16)
            vh = v_refs[i][pl.ds(h, page, stride=n_heads), :].astype(BF16)
            acc = acc + jnp.dot(ph, vh, preferred_element_type=F32)
    acc_ref[...] = acc

    @pl.when(pg == pl.num_programs(1) - 1)
    def _():
        lam = _lambda_value(lq1_ref, lk1_ref, lq2_ref, lk2_ref, lam_init)
        q_all, kn_all, vn_all = lhs[0], None, None
        for h in range(1, n_heads):
            q_all = q_all + lhs[h]
        for h in range(n_heads):
            sel = rowcol // 2 == h
            kn_h = jnp.where(sel, kn_ref[h:h + 1, :], 0.0)
            vn_h = jnp.where(sel, vn_ref[h:h + 1, :], 0.0)
            kn_all = kn_h if kn_all is None else kn_all + kn_h
            vn_all = vn_h if vn_all is None else vn_all + vn_h
        s_self = jnp.sum(q_all * kn_all, axis=1, keepdims=True)
        m1 = m_ref[...]
        m2 = jnp.maximum(m1, s_self)
        a1 = jnp.exp(m1 - m2)
        p_self = jnp.exp(s_self - m2)
        a = (a1 * acc_ref[...] + p_self * vn_all) / (a1 * l_ref[...] + p_self)
        for h in range(n_heads):
            o_ref[h:h + 1, :] = _diff_finish(a[2 * h:2 * h + 1, :], a[2 * h + 1:2 * h + 2, :], lam,
                                             subln_ref[...], lam_init)


def _diff_sample(pool_k, pool_v, page_table, us4, lams, subln, lam_init, pps=8):
    n_dec, n_pages = page_table.shape
    n_heads = us4.shape[2]
    rows = pool_k.shape[1]
    assert n_pages % pps == 0
    vec = pl.BlockSpec((1, HEAD_DIM), lambda b, p, pt: (0, 0))
    pages = [pl.BlockSpec((None, rows, LANES), lambda b, p, pt, i=i: (pt[b, pps * p + i], 0, 0)) for i in range(pps)]
    grid_spec = pltpu.PrefetchScalarGridSpec(
        num_scalar_prefetch=1, grid=(n_dec, n_pages // pps),
        in_specs=[
            pl.BlockSpec((None, None, n_heads, LANES), lambda b, p, pt: (b, 0, 0, 0)),
            pl.BlockSpec((None, None, n_heads, LANES), lambda b, p, pt: (b, 1, 0, 0)),
            pl.BlockSpec((None, None, n_heads, LANES), lambda b, p, pt: (b, 2, 0, 0)),
            vec, vec, vec, vec,
            pl.BlockSpec((1, LANES), lambda b, p, pt: (0, 0)),
        ] + pages + pages,
        out_specs=pl.BlockSpec((None, n_heads, LANES), lambda b, p, pt: (b, 0, 0)),
        scratch_shapes=[pltpu.VMEM((2 * n_heads, 1), F32), pltpu.VMEM((2 * n_heads, 1), F32),
                        pltpu.VMEM((2 * n_heads, LANES), F32)],
    )
    body = functools.partial(_diff_sample_body, n_heads=n_heads, page=rows // n_heads, pps=pps, lam_init=lam_init)
    out = pl.pallas_call(
        body, grid_spec=grid_spec,
        out_shape=jax.ShapeDtypeStruct((n_dec, n_heads, LANES), F32),
        compiler_params=_params(2), name="diff_sample",
    )(page_table, us4, us4, us4, *lams, subln, *([pool_k] * pps), *([pool_v] * pps))
    return out.reshape(n_dec, D_MODEL)


def _dilated_sample_body(qkv_ref, new_ref, c0_ref, c1_ref, c2_ref, o_ref, n0_ref, n1_ref, n2_ref, *, hps):
    b = pl.program_id(0)
    c_refs = (c0_ref, c1_ref, c2_ref)
    n_refs = (n0_ref, n1_ref, n2_ref)
    n_lane = lax.broadcasted_iota(jnp.int32, (1, new_ref.shape[-1]), 1)
    for h in range(hps):
        ms, ls, accs = [], [], []
        for g, (win, dil) in enumerate(C_PATTERNS):
            rows = c_refs[g].shape[-1]
            q = jnp.broadcast_to(qkv_ref[3 * g, h:h + 1, :] * SCALE, (8, HEAD_DIM))
            kn = qkv_ref[3 * g + 1, h:h + 1, :]
            vn = qkv_ref[3 * g + 2, h:h + 1, :]
            kt = c_refs[g][0, h]
            vt = c_refs[g][1, h]
            lane = lax.broadcasted_iota(jnp.int32, (1, rows), 1)
            sc = jnp.dot(q.astype(BF16), kt.astype(BF16), preferred_element_type=F32)
            if dil > 1:
                sc = jnp.where(lane % dil == 0, sc, NEG)
            s_self = jnp.sum(q * kn, axis=1, keepdims=True)
            m = jnp.maximum(jnp.max(sc, axis=1, keepdims=True), s_self)
            p = jnp.exp(sc - m)
            p_self = jnp.exp(s_self - m)
            ms.append(m)
            ls.append(jnp.sum(p, axis=1, keepdims=True) + p_self)
            accs.append(lax.dot_general(p.astype(BF16), vt.astype(BF16), _ABT, preferred_element_type=F32)
                        + p_self * vn)
            for kv, x in ((0, kt), (1, vt)):
                col = jnp.sum(jnp.where(n_lane == b, new_ref[g, kv, h], 0.0), axis=1, keepdims=True)
                n_refs[g][kv, h] = jnp.where(lane == rows - 1, col, pltpu.roll(x, rows - 1, 1))
        mx = jnp.maximum(jnp.maximum(ms[0], ms[1]), ms[2])
        w = [jnp.exp(ms[g] - mx) for g in range(3)]
        den = w[0] * ls[0] + w[1] * ls[1] + w[2] * ls[2]
        o = (w[0] * accs[0] + w[1] * accs[1] + w[2] * accs[2]) / den
        o_ref[h:h + 1, :] = o[0:1, :]


def _dilated_sample(qkv, new_t, caches_t, hps=4):
    n_dec = qkv.shape[0]
    heads = caches_t[0].shape[2]
    hb = heads // hps
    in_specs = [
        pl.BlockSpec((None, 9, None, hps, HEAD_DIM), lambda b, s: (b, 0, s, 0, 0)),
        pl.BlockSpec((3, 2, hps, HEAD_DIM, n_dec), lambda b, s: (0, 0, s, 0, 0)),
    ]
    cache_specs = [pl.BlockSpec((None, 2, hps, HEAD_DIM, c.shape[-1]), lambda b, s: (b, 0, s, 0, 0)) for c in caches_t]
    outs = pl.pallas_call(
        functools.partial(_dilated_sample_body, hps=hps), grid=(n_dec, hb),
        in_specs=in_specs + cache_specs,
        out_specs=[pl.BlockSpec((None, None, hps, HEAD_DIM), lambda b, s: (b, s, 0, 0))] + cache_specs,
        out_shape=[jax.ShapeDtypeStruct((n_dec, hb, hps, HEAD_DIM), F32)]
        + [jax.ShapeDtypeStruct(c.shape, F32) for c in caches_t],
        compiler_params=_params(2), name="dilated_sample",
    )(qkv, new_t, *caches_t)
    return outs[0].reshape(n_dec, heads * HEAD_DIM), outs[1:]


def _rope_tables(pos):
    half = HEAD_DIM // 2
    inv = ROPE_THETA ** (-jnp.arange(half, dtype=F32) / half)
    ang = pos.astype(F32)[:, None] * inv[None, :]
    cos, sin = jnp.cos(ang), jnp.sin(ang)
    cos = jnp.concatenate([cos, cos], axis=1)
    sin = jnp.concatenate([-sin, sin], axis=1)
    return jnp.tile(cos, (1, LANES // HEAD_DIM)), jnp.tile(sin, (1, LANES // HEAD_DIM))


def _head_gain(g):
    return jnp.tile(g.astype(F32), D_MODEL // HEAD_DIM)


def kernel(x_prompt, x_sample, cache_a_k, cache_a_v, cache_b_k, cache_b_v, cache_c_kv0, cache_c_kv1, cache_c_kv2,
           page_table, norm_a, w_in_a, qn_a, kn_a, w_out_a, norm_b, w_in_b, qn_b, kn_b, lam_q1_b, lam_k1_b,
           lam_q2_b, lam_k2_b, subln_b, w_out_b, norm_c, w_in_c, qn_c, kn_c, w_out_c):
    n, seq, _ = x_prompt.shape
    n_dec, dec_seq, _ = x_sample.shape
    assert dec_seq == 1 and seq % 512 == 0
    page = cache_a_k.shape[2]
    n_pages = page_table.shape[1]
    past = n_pages * page
    n_pool = cache_a_k.shape[1]
    depth = norm_a.shape[0] + norm_b.shape[0] + norm_c.shape[0]
    tm = 1024
    a_heads = D_MODEL // HEAD_DIM
    b_heads = D_MODEL // (2 * HEAD_DIM)
    c_caches = (cache_c_kv0, cache_c_kv1, cache_c_kv2)

    cos_p, sin_p = _rope_tables(jnp.arange(seq, dtype=jnp.int32))
    cos_s, sin_s = _rope_tables(jnp.full((n_dec,), past, jnp.int32))
    bd = (jnp.arange(256)[:, None] // HEAD_DIM == jnp.arange(256)[None, :] // HEAD_DIM).astype(BF16)
    zero_gain = jnp.zeros((D_MODEL,), F32)
    cache_a_kt = jnp.transpose(cache_a_k, (0, 1, 3, 4, 2))
    cache_a_vt = jnp.transpose(cache_a_v, (0, 1, 3, 4, 2))

    hp = x_prompt.reshape(n * seq, D_MODEL)
    hs = x_sample.reshape(n_dec, D_MODEL)
    ak_p, av_p, ak_s, av_s = [], [], [], []
    bk_p, bv_p, bk_s, bv_s = [], [], [], []
    ckv_p = [[] for _ in C_PATTERNS]
    ckv_s = [[] for _ in C_PATTERNS]

    def project_both(norm, w_in, gains, modes):
        w_bf = w_in.astype(BF16)
        gains = gains.reshape(gains.shape[0], 1, COL_TILE)
        up = _project(hp, norm, w_bf, gains, modes, cos_p, sin_p, bd, tm)
        us = _project(hs, norm, w_bf, gains, modes, cos_s, sin_s, bd, n_dec)
        return up, us

    for layer in range(depth):
        kind, j = layer % 3, layer // 3
        if kind == 0:
            gains = jnp.stack([_head_gain(qn_a[j]), _head_gain(kn_a[j]), zero_gain, zero_gain])
            modes = jnp.array([1, 1, 0, 0], jnp.int32)
            up, us = project_both(norm_a[j], w_in_a[j], gains, modes)
            op = _moba_prompt(up.reshape(n, seq, 4 * D_MODEL)).reshape(n * seq, D_MODEL)
            us3 = us.reshape(n_dec, 4, a_heads, HEAD_DIM)
            kmean = _moba_kmean(cache_a_kt, j, page_table)
            sel = _moba_select(kmean, us.reshape(n_dec, 1, 4 * D_MODEL), a_heads)
            sel = jnp.swapaxes(sel[:, :, :MOBA_TOPK], 1, 2).reshape(n_dec, MOBA_TOPK * a_heads)
            os_ = _moba_sample(cache_a_kt, cache_a_vt, j, page_table, sel, us3[:, 0], us3[:, 1], us3[:, 2])
            hp = _out_project(op, up, hp, w_out_a[j].astype(BF16), tm)
            hs = _out_project(os_, us, hs, w_out_a[j].astype(BF16), n_dec)
            ak_p.append(up[:, D_MODEL:2 * D_MODEL].reshape(n, seq, a_heads, HEAD_DIM))
            av_p.append(up[:, 2 * D_MODEL:3 * D_MODEL].reshape(n, seq, a_heads, HEAD_DIM))
            ak_s.append(us[:, D_MODEL:2 * D_MODEL].reshape(n_dec, 1, a_heads, HEAD_DIM))
            av_s.append(us[:, 2 * D_MODEL:3 * D_MODEL].reshape(n_dec, 1, a_heads, HEAD_DIM))
        elif kind == 1:
            lam_init = 0.8 - 0.6 * math.exp(-0.3 * layer)
            gains = jnp.stack([_head_gain(qn_b[j]), _head_gain(kn_b[j]), zero_gain, zero_gain])
            modes = jnp.array([1, 1, 0, 0], jnp.int32)
            up, us = project_both(norm_b[j], w_in_b[j], gains, modes)
            lams = tuple(v[j].astype(F32).reshape(1, HEAD_DIM) for v in (lam_q1_b, lam_k1_b, lam_q2_b, lam_k2_b))
            subln = subln_b[j].astype(F32).reshape(1, LANES)
            op = _diff_prompt(up.reshape(n, seq, 4 * D_MODEL), lams, subln, lam_init).reshape(n * seq, D_MODEL)
            pool_k = cache_b_k[j].reshape(n_pool, page * b_heads, LANES)
            pool_v = cache_b_v[j].reshape(n_pool, page * b_heads, LANES)
            os_ = _diff_sample(pool_k, pool_v, page_table, us.reshape(n_dec, 4, b_heads, LANES), lams, subln, lam_init)
            hp = _out_project(op, up, hp, w_out_b[j].astype(BF16), tm)
            hs = _out_project(os_, us, hs, w_out_b[j].astype(BF16), n_dec)
            bk_p.append(up[:, D_MODEL:2 * D_MODEL].reshape(n, seq, b_heads, LANES))
            bv_p.append(up[:, 2 * D_MODEL:3 * D_MODEL].reshape(n, seq, b_heads, LANES))
            bk_s.append(us[:, D_MODEL:2 * D_MODEL].reshape(n_dec, 1, b_heads, LANES))
            bv_s.append(us[:, 2 * D_MODEL:3 * D_MODEL].reshape(n_dec, 1, b_heads, LANES))
        else:
            gain_rows, mode_vals = [], []
            for g in range(len(C_PATTERNS)):
                gain_rows += [_head_gain(qn_c[j, g]), _head_gain(kn_c[j, g]), zero_gain]
                mode_vals += [1, 1, 0]
            gains = jnp.stack(gain_rows + [zero_gain])
            modes = jnp.array(mode_vals + [0], jnp.int32)
            up, us = project_both(norm_c[j], w_in_c[j], gains, modes)
            e = up.shape[1]
            op = _dilated_prompt(up.reshape(n, seq, e)).reshape(n * seq, D_MODEL)
            n_grp = len(C_PATTERNS)
            hps = 4
            qkv = us[:, :3 * n_grp * D_MODEL].reshape(n_dec, 3 * n_grp, a_heads // hps, hps, HEAD_DIM)
            new_kv = us[:, :3 * n_grp * D_MODEL].reshape(n_dec, n_grp, 3, a_heads, HEAD_DIM)[:, :, 1:]
            new_t = jnp.transpose(new_kv, (1, 2, 3, 4, 0))
            caches_t = []
            for g, (win, dil) in enumerate(C_PATTERNS):
                assert c_caches[g].shape[2] == win and win // dil == 128
                caches_t.append(jnp.transpose(c_caches[g][j], (0, 2, 3, 4, 1)))
            os_, adv = _dilated_sample(qkv, new_t, caches_t, hps)
            hp = _out_project(op, up, hp, w_out_c[j].astype(BF16), tm)
            hs = _out_project(os_, us, hs, w_out_c[j].astype(BF16), n_dec)
            up3 = up.reshape(n, seq, e)
            for g, (win, dil) in enumerate(C_PATTERNS):
                c0 = (3 * g + 1) * D_MODEL
                keep_p = min(win, seq)
                ckv_p[g].append(up3[:, seq - keep_p:, c0:c0 + 2 * D_MODEL].reshape(n, keep_p, 2, a_heads, HEAD_DIM))
                ckv_s[g].append(jnp.transpose(adv[g], (0, 4, 1, 2, 3)))

    return (hp.reshape(n, seq, D_MODEL), hs.reshape(n_dec, 1, D_MODEL),
            jnp.stack(ak_p), jnp.stack(av_p), jnp.stack(ak_s), jnp.stack(av_s),
            jnp.stack(bk_p), jnp.stack(bv_p), jnp.stack(bk_s), jnp.stack(bv_s),
            jnp.stack(ckv_p[0]), jnp.stack(ckv_p[1]), jnp.stack(ckv_p[2]),
            jnp.stack(ckv_s[0]), jnp.stack(ckv_s[1]), jnp.stack(ckv_s[2]))
```

```python
import functools
import math

import jax
import jax.numpy as jnp
from jax import lax
from jax.experimental import pallas as pl
from jax.experimental.pallas import tpu as pltpu

F32 = jnp.float32
BF16 = jnp.bfloat16

D_MODEL = 1024
HEAD_DIM = 64
LANES = 128
COL_TILE = 1024
EPS = 1e-6
ROPE_THETA = 10000.0
SCALE = HEAD_DIM ** -0.5
SCALE_LOG2E = SCALE * math.log2(math.e)
NEG = -1e30
MOBA_BLOCK = 256
MOBA_TOPK = 3
C_PATTERNS = ((128, 1), (512, 4), (2048, 16))
VMEM_LIMIT = 48 * 1024 * 1024

_ABT = (((1,), (1,)), ((), ()))


def _params(n_axes):
    return pltpu.CompilerParams(dimension_semantics=("arbitrary",) * n_axes,
                                vmem_limit_bytes=VMEM_LIMIT)


def _lane_lo():
    return lax.broadcasted_iota(jnp.int32, (1, LANES), 1) < HEAD_DIM


def _row_picks_head():
    row = lax.broadcasted_iota(jnp.int32, (8, LANES), 0)
    lane = lax.broadcasted_iota(jnp.int32, (8, LANES), 1)
    return (lane // HEAD_DIM) == row


def _proj_body(mode_ref, x_ref, g_ref, w_ref, gain_ref, cos_ref, sin_ref, bd_ref, o_ref, xn_ref):
    j = pl.program_id(1)

    @pl.when(j == 0)
    def _():
        x = x_ref[...]
        ms = jnp.mean(x * x, axis=-1, keepdims=True)
        xn_ref[...] = (x * lax.rsqrt(ms + EPS) * g_ref[...]).astype(BF16)

    u = jnp.dot(xn_ref[...], w_ref[...], preferred_element_type=F32)
    mode = mode_ref[j]

    @pl.when(mode == 0)
    def _():
        o_ref[...] = u

    @pl.when(mode == 1)
    def _():
        gain = gain_ref[...]
        cos = cos_ref[...]
        sin = sin_ref[...]
        lane = lax.broadcasted_iota(jnp.int32, (1, LANES), 1)
        first_half = (lane % HEAD_DIM) < (HEAD_DIM // 2)
        for c in range(COL_TILE // 256):
            uc = u[:, c * 256:(c + 1) * 256]
            ss = jnp.dot((uc * uc).astype(BF16), bd_ref[...], preferred_element_type=F32)
            y = uc * lax.rsqrt(ss * (1.0 / HEAD_DIM) + EPS) * gain[:, c * 256:(c + 1) * 256]
            for s in range(2):
                ys = y[:, s * LANES:(s + 1) * LANES]
                rot = jnp.where(first_half, pltpu.roll(ys, 96, 1), pltpu.roll(ys, 32, 1))
                o_ref[:, c * 256 + s * LANES:c * 256 + (s + 1) * LANES] = ys * cos + rot * sin


def _project(x, g, w_bf, gains, modes, cos, sin, bd, tm):
    t, e = x.shape[0], w_bf.shape[1]
    nt = e // COL_TILE
    npb = cos.shape[0] // tm
    grid_spec = pltpu.PrefetchScalarGridSpec(
        num_scalar_prefetch=1,
        grid=(t // tm, nt),
        in_specs=[
            pl.BlockSpec((tm, D_MODEL), lambda i, j, m: (i, 0)),
            pl.BlockSpec((1, D_MODEL), lambda i, j, m: (0, 0)),
            pl.BlockSpec((D_MODEL, COL_TILE), lambda i, j, m: (0, j)),
            pl.BlockSpec((None, 1, COL_TILE), lambda i, j, m: (j, 0, 0)),
            pl.BlockSpec((tm, LANES), lambda i, j, m: (i % npb, 0)),
            pl.BlockSpec((tm, LANES), lambda i, j, m: (i % npb, 0)),
            pl.BlockSpec((256, 256), lambda i, j, m: (0, 0)),
        ],
        out_specs=pl.BlockSpec((tm, COL_TILE), lambda i, j, m: (i, j)),
        scratch_shapes=[pltpu.VMEM((tm, D_MODEL), BF16)],
    )
    return pl.pallas_call(
        _proj_body, grid_spec=grid_spec,
        out_shape=jax.ShapeDtypeStruct((t, e), F32),
        compiler_params=_params(2), name="proj_in",
    )(modes, x, g.reshape(1, D_MODEL), w_bf, gains, cos, sin, bd)


def _outproj_body(o_ref, gate_ref, h_ref, w_ref, y_ref):
    g = gate_ref[...]
    z = o_ref[...] * (g * jax.nn.sigmoid(g))
    y_ref[...] = h_ref[...] + jnp.dot(z.astype(BF16), w_ref[...], preferred_element_type=F32)


def _out_project(o, u, h, w_bf, tm):
    t = o.shape[0]
    gate_blk = u.shape[1] // COL_TILE - 1
    return pl.pallas_call(
        _outproj_body, grid=(t // tm,),
        in_specs=[
            pl.BlockSpec((tm, D_MODEL), lambda i: (i, 0)),
            pl.BlockSpec((tm, COL_TILE), lambda i: (i, gate_blk)),
            pl.BlockSpec((tm, D_MODEL), lambda i: (i, 0)),
            pl.BlockSpec((D_MODEL, D_MODEL), lambda i: (0, 0)),
        ],
        out_specs=pl.BlockSpec((tm, D_MODEL), lambda i: (i, 0)),
        out_shape=jax.ShapeDtypeStruct((t, D_MODEL), F32),
        compiler_params=_params(1), name="proj_out",
    )(o, u, h, w_bf)


def _two_pass_rows(qm, qi, rowpos, kb_ref, vb_ref, s_ref, m_ref, l_ref, acc_ref, *, nq, tq, tk):
    half = tk // 2
    per = tq // tk

    def score_tiles(n_past, n_all):
        for h in range(2):
            tmax = None
            k_ref = kb_ref[h] if isinstance(kb_ref, (tuple, list)) else kb_ref
            for kb in range(n_all):
                s = lax.dot_general(qm[h], k_ref[kb * tk:(kb + 1) * tk, :], _ABT, preferred_element_type=F32)
                if kb >= n_past:
                    colpos = kb * tk + lax.broadcasted_iota(jnp.int32, (1, tk), 1)
                    s = jnp.where(colpos <= rowpos, s, NEG)
                s_ref[h, kb] = s
                t = jnp.maximum(s[:, :half], s[:, half:])
                tmax = t if tmax is None else jnp.maximum(tmax, t)
            m_ref[h] = jnp.max(tmax, axis=1, keepdims=True)

    for v in range(nq):
        pl.when(qi == v)(functools.partial(score_tiles, v * per, (v + 1) * per))

    m = [m_ref[h] for h in range(2)]

    def value_tiles(n_all):
        for h in range(2):
            psum, acc = None, None
            for kb in range(n_all):
                p = jnp.exp2(s_ref[h, kb] - m[h])
                t = p[:, :half] + p[:, half:]
                psum = t if psum is None else psum + t
                d = jnp.dot(p.astype(BF16), vb_ref[kb * tk:(kb + 1) * tk, :], preferred_element_type=F32)
                acc = d if acc is None else acc + d
            l_ref[h] = jnp.sum(psum, axis=1, keepdims=True)
            acc_ref[h] = acc

    for v in range(nq):
        pl.when(qi == v)(functools.partial(value_tiles, (v + 1) * per))

    return [acc_ref[h] / l_ref[h] for h in range(2)]


def _two_pass_scratch(seq, tq, tk):
    return [
        pltpu.VMEM((seq, LANES), BF16), pltpu.VMEM((seq, LANES), BF16),
        pltpu.VMEM((2, seq // tk, tq, tk), F32),
        pltpu.VMEM((2, tq, 1), F32), pltpu.VMEM((2, tq, 1), F32), pltpu.VMEM((2, tq, LANES), F32),
    ]


def _moba_prompt_body(q_ref, k_ref, v_ref, o_ref, ke_ref, vb_ref, s_ref, m_ref, l_ref, acc_ref, ko_ref, bias_ref, *, seq, tq):
    tb = MOBA_BLOCK
    nb = seq // MOBA_BLOCK
    lo = _lane_lo()
    lane = lax.broadcasted_iota(jnp.int32, (1, LANES), 1)
    k = k_ref[...]
    kblock = lax.broadcasted_iota(jnp.int32, (seq, 1), 0) // MOBA_BLOCK
    ke_ref[...] = jnp.where(lo, k, jnp.where(lane - HEAD_DIM == kblock, 1.0, 0.0)).astype(BF16)
    ko_ref[...] = jnp.where(lo, jnp.where(lane == kblock, 1.0, 0.0), k).astype(BF16)
    vb_ref[...] = v_ref[...].astype(BF16)
    kmean = jnp.concatenate(
        [jnp.mean(k_ref[b * MOBA_BLOCK:(b + 1) * MOBA_BLOCK, :], axis=0, keepdims=True) for b in range(nb)],
        axis=0)
    kcat = jnp.concatenate([jnp.where(lo, kmean, 0.0), jnp.where(lo, 0.0, kmean)], axis=0)

    gate = lax.dot_general(kcat, q_ref[...], _ABT, precision=lax.Precision.HIGHEST, preferred_element_type=F32)
    bl = lax.broadcasted_iota(jnp.int32, (nb, seq), 0)
    own = lax.broadcasted_iota(jnp.int32, (1, seq), 1) // MOBA_BLOCK
    bias_t = []
    for h in range(2):
        gh = gate[h * nb:(h + 1) * nb, :]
        cnt = jnp.zeros((nb, seq), F32)
        for b2 in range(nb):
            row = gh[b2:b2 + 1, :]
            beats = jnp.where(row > gh, 1.0, jnp.where(row == gh, jnp.where(bl > b2, 1.0, 0.0), 0.0))
            cnt = cnt + jnp.where(own > b2, beats, 0.0)
        bias_t.append(jnp.where(bl < own, jnp.where(cnt < MOBA_TOPK, 0.0, NEG), 0.0))
    pad = jnp.zeros((HEAD_DIM - nb, seq), F32)
    bias_t = jnp.concatenate([bias_t[1], pad, bias_t[0], pad], axis=0).astype(BF16)
    eye = (lax.broadcasted_iota(jnp.int32, (tb, tb), 0) == lax.broadcasted_iota(jnp.int32, (tb, tb), 1)).astype(BF16)
    for qb in range(seq // tb):
        bias_ref[qb * tb:(qb + 1) * tb, :] = lax.dot_general(eye, bias_t[:, qb * tb:(qb + 1) * tb], _ABT,
                                                             preferred_element_type=F32)

    def qblock(qi, carry):
        r0 = pl.multiple_of(qi * tq, tq)
        qs = q_ref[pl.ds(r0, tq), :] * SCALE_LOG2E
        bias = bias_ref[pl.ds(r0, tq), :]
        qm = (jnp.where(lo, qs, bias).astype(BF16), jnp.where(lo, bias, qs).astype(BF16))
        rowpos = r0 + lax.broadcasted_iota(jnp.int32, (tq, 1), 0)
        outs = _two_pass_rows(qm, qi, rowpos, (ke_ref, ko_ref), vb_ref, s_ref, m_ref, l_ref, acc_ref,
                              nq=seq // tq, tq=tq, tk=MOBA_BLOCK)
        o_ref[pl.ds(r0, tq), :] = jnp.where(lo, outs[0], outs[1])
        return carry

    lax.fori_loop(0, seq // tq, qblock, 0)


def _moba_prompt(u3, tq=512):
    n, seq, _ = u3.shape
    nslab = D_MODEL // LANES
    body = functools.partial(_moba_prompt_body, seq=seq, tq=tq)
    return pl.pallas_call(
        body, grid=(n, nslab),
        in_specs=[
            pl.BlockSpec((None, seq, LANES), lambda b, s: (b, 0, s)),
            pl.BlockSpec((None, seq, LANES), lambda b, s: (b, 0, nslab + s)),
            pl.BlockSpec((None, seq, LANES), lambda b, s: (b, 0, 2 * nslab + s)),
        ],
        out_specs=pl.BlockSpec((None, seq, LANES), lambda b, s: (b, 0, s)),
        out_shape=jax.ShapeDtypeStruct((n, seq, D_MODEL), F32),
        scratch_shapes=_two_pass_scratch(seq, tq, MOBA_BLOCK)
        + [pltpu.VMEM((seq, LANES), BF16), pltpu.VMEM((seq, LANES), F32)],
        compiler_params=_params(2), name="moba_prompt",
    )(u3, u3, u3)


def _lambda_value(lq1_ref, lk1_ref, lq2_ref, lk2_ref, lam_init):
    a = jnp.sum(lq1_ref[...] * lk1_ref[...], axis=1, keepdims=True)
    b = jnp.sum(lq2_ref[...] * lk2_ref[...], axis=1, keepdims=True)
    return jnp.exp(a) - jnp.exp(b) + lam_init


def _diff_finish(a1, a2, lam, subln, lam_init):
    o = a1 - lam * a2
    ms = jnp.mean(o * o, axis=-1, keepdims=True)
    return (o * lax.rsqrt(ms + EPS) * subln) * (1.0 - lam_init)


def _diff_prompt_body(q_ref, k_ref, v_ref, lq1_ref, lk1_ref, lq2_ref, lk2_ref, subln_ref, o_ref,
                      kb_ref, vb_ref, s_ref, m_ref, l_ref, acc_ref, *, seq, tq, tk, lam_init):
    lo = _lane_lo()
    kb_ref[...] = k_ref[...].astype(BF16)
    vb_ref[...] = v_ref[...].astype(BF16)
    lam = _lambda_value(lq1_ref, lk1_ref, lq2_ref, lk2_ref, lam_init)
    subln = subln_ref[...]

    def qblock(qi, carry):
        r0 = pl.multiple_of(qi * tq, tq)
        qs = q_ref[pl.ds(r0, tq), :] * SCALE_LOG2E
        qm = (jnp.where(lo, qs, 0.0).astype(BF16), jnp.where(lo, 0.0, qs).astype(BF16))
        rowpos = r0 + lax.broadcasted_iota(jnp.int32, (tq, 1), 0)
        outs = _two_pass_rows(qm, qi, rowpos,
                              kb_ref, vb_ref, s_ref, m_ref, l_ref, acc_ref, nq=seq // tq, tq=tq, tk=tk)
        o_ref[pl.ds(r0, tq), :] = _diff_finish(outs[0], outs[1], lam, subln, lam_init)
        return carry

    lax.fori_loop(0, seq // tq, qblock, 0)


def _diff_prompt(u3, lams, subln, lam_init, tq=512, tk=256):
    n, seq, _ = u3.shape
    nslab = D_MODEL // LANES
    body = functools.partial(_diff_prompt_body, seq=seq, tq=tq, tk=tk, lam_init=lam_init)
    vec = pl.BlockSpec((1, HEAD_DIM), lambda b, s: (0, 0))
    return pl.pallas_call(
        body, grid=(n, nslab),
        in_specs=[
            pl.BlockSpec((None, seq, LANES), lambda b, s: (b, 0, s)),
            pl.BlockSpec((None, seq, LANES), lambda b, s: (b, 0, nslab + s)),
            pl.BlockSpec((None, seq, LANES), lambda b, s: (b, 0, 2 * nslab + s)),
            vec, vec, vec, vec,
            pl.BlockSpec((1, LANES), lambda b, s: (0, 0)),
        ],
        out_specs=pl.BlockSpec((None, seq, LANES), lambda b, s: (b, 0, s)),
        out_shape=jax.ShapeDtypeStruct((n, seq, D_MODEL), F32),
        scratch_shapes=_two_pass_scratch(seq, tq, tk),
        compiler_params=_params(2), name="diff_prompt",
    )(u3, u3, u3, *lams, subln)


def _dilated_prompt_body(*refs, seq, unroll):
    qkv_refs = refs[:9]
    o_ref = refs[9]
    qc_ref, kc_ref, vc_ref, ocm_ref, lcm_ref, og_ref, lg_ref = refs[10:]
    lo = _lane_lo()
    blk = 128
    nblk = seq // blk
    ri = lax.broadcasted_iota(jnp.int32, (blk, 2 * blk), 0)
    ci = lax.broadcasted_iota(jnp.int32, (blk, 2 * blk), 1)
    bias_band = jnp.where((ci >= ri) & (ci <= ri + blk), 0.0, NEG)
    bias_own = jnp.where((ci >= blk) & (ci <= ri + blk), 0.0, NEG)
    kc_ref[0:blk, :] = jnp.zeros((blk, LANES), BF16)
    vc_ref[0:blk, :] = jnp.zeros((blk, LANES), BF16)

    def attend(qb, kk, vv, bias):
        outs, lses = [], []
        for h in range(2):
            qm = jnp.where(lo, qb, 0.0) if h == 0 else jnp.where(lo, 0.0, qb)
            s = lax.dot_general(qm.astype(BF16), kk, _ABT, preferred_element_type=F32) + bias
            m = jnp.max(s, axis=1, keepdims=True)
            p = jnp.exp(s - m)
            l = jnp.sum(p, axis=1, keepdims=True)
            outs.append(jnp.dot(p.astype(BF16), vv, preferred_element_type=F32) / l)
            lses.append(m + jnp.log(l))
        return jnp.where(lo, outs[0], outs[1]), jnp.where(lo, lses[0], lses[1])

    for g, (win, dil) in enumerate(C_PATTERNS):
        q_ref, k_ref, v_ref = qkv_refs[3 * g:3 * g + 3]
        ncls = seq // dil
        bpc = ncls // blk
        for r in range(dil):
            rows = pl.ds(r, ncls, stride=dil) if dil > 1 else slice(None)
            qc_ref[r * ncls:(r + 1) * ncls, :] = q_ref[rows, :] * SCALE
            kc_ref[blk + r * ncls:blk + (r + 1) * ncls, :] = k_ref[rows, :].astype(BF16)
            vc_ref[blk + r * ncls:blk + (r + 1) * ncls, :] = v_ref[rows, :].astype(BF16)

        def blocks(it, carry):
            for u in range(unroll):
                gb = it * unroll + u
                r0 = pl.multiple_of(gb * blk, blk)
                bias = jnp.where((gb % bpc) != 0, bias_band, bias_own)
                o, lse = attend(qc_ref[pl.ds(r0, blk), :], kc_ref[pl.ds(r0, 2 * blk), :],
                                vc_ref[pl.ds(r0, 2 * blk), :], bias)
                ocm_ref[pl.ds(r0, blk), :] = o
                lcm_ref[pl.ds(r0, blk), :] = lse
            return carry

        lax.fori_loop(0, nblk // unroll, blocks, 0)
        for r in range(dil):
            rows = pl.ds(r, ncls, stride=dil) if dil > 1 else slice(None)
            src = slice(r * ncls, (r + 1) * ncls)
            og_ref[g, rows, :] = ocm_ref[src, :]
            lg_ref[g, rows, :] = lcm_ref[src, :]

    lmax = jnp.maximum(jnp.maximum(lg_ref[0], lg_ref[1]), lg_ref[2])
    w = [jnp.exp(lg_ref[g] - lmax) for g in range(3)]
    den = w[0] + w[1] + w[2]
    o_ref[...] = (w[0] * og_ref[0] + w[1] * og_ref[1] + w[2] * og_ref[2]) / den


def _dilated_prompt(u3):
    n, seq, _ = u3.shape
    nslab = D_MODEL // LANES
    in_specs = []
    for g in range(3):
        for c in range(3):
            base = (3 * g + c) * nslab
            in_specs.append(pl.BlockSpec((None, seq, LANES), lambda b, s, base=base: (b, 0, base + s)))
    body = functools.partial(_dilated_prompt_body, seq=seq, unroll=16)
    return pl.pallas_call(
        body, grid=(n, nslab),
        in_specs=in_specs,
        out_specs=pl.BlockSpec((None, seq, LANES), lambda b, s: (b, 0, s)),
        out_shape=jax.ShapeDtypeStruct((n, seq, D_MODEL), F32),
        scratch_shapes=[
            pltpu.VMEM((seq, LANES), F32), pltpu.VMEM((seq + 128, LANES), BF16), pltpu.VMEM((seq + 128, LANES), BF16),
            pltpu.VMEM((seq, LANES), F32), pltpu.VMEM((seq, LANES), F32),
            pltpu.VMEM((3, seq, LANES), F32), pltpu.VMEM((3, seq, LANES), F32),
        ],
        compiler_params=_params(2), name="dilated_prompt",
    )(*([u3] * 9))


def _kmean_body(pt_ref, *refs, pages_per_block, pps):
    k_refs, o_ref = refs[:pps], refs[pps]
    p = pl.program_id(1)
    heads, dim, page = k_refs[0].shape
    bps = pps // pages_per_block
    lane = lax.broadcasted_iota(jnp.int32, (1, o_ref.shape[-1]), 1)

    @pl.when(p == 0)
    def _():
        o_ref[...] = jnp.zeros(o_ref.shape, F32)

    for b in range(bps):
        s = k_refs[b * pages_per_block][...]
        for i in range(1, pages_per_block):
            s = s + k_refs[b * pages_per_block + i][...]
        col = jnp.sum(s.reshape(heads * dim, page), axis=1, keepdims=True) * (1.0 / (pages_per_block * page))
        o_ref[...] = jnp.where(lane == p * bps + b, col, o_ref[...])


def _moba_kmean(cache_t, layer, page_table, pps=8):
    n_dec, n_pages = page_table.shape
    _, _, heads, dim, page = cache_t.shape
    ppb = MOBA_BLOCK // page
    assert pps % ppb == 0 and n_pages % pps == 0
    pages = [pl.BlockSpec((None, None, heads, dim, page), lambda b, p, pt, i=i: (layer, pt[b, pps * p + i], 0, 0, 0))
             for i in range(pps)]
    grid_spec = pltpu.PrefetchScalarGridSpec(
        num_scalar_prefetch=1, grid=(n_dec, n_pages // pps),
        in_specs=pages,
        out_specs=pl.BlockSpec((None, heads * dim, n_pages // ppb), lambda b, p, pt: (b, 0, 0)),
    )
    return pl.pallas_call(
        functools.partial(_kmean_body, pages_per_block=ppb, pps=pps), grid_spec=grid_spec,
        out_shape=jax.ShapeDtypeStruct((n_dec, heads * dim, n_pages // ppb), F32),
        compiler_params=_params(2), name="moba_kmean",
    )(page_table, *([cache_t] * pps))


def _moba_select_body(km_ref, q_ref, o_ref):
    nbk = km_ref.shape[1]
    n_heads = o_ref.shape[0]
    row = lax.broadcasted_iota(jnp.int32, (n_heads, D_MODEL), 0)
    col = lax.broadcasted_iota(jnp.int32, (n_heads, D_MODEL), 1)
    q_blk = jnp.where(col // HEAD_DIM == row, q_ref[...], 0.0)
    gate = jnp.dot(q_blk, km_ref[...], precision=lax.Precision.HIGHEST, preferred_element_type=F32)
    blk = lax.broadcasted_iota(jnp.int32, (n_heads, nbk), 1)
    out_lane = lax.broadcasted_iota(jnp.int32, o_ref.shape, 1)
    sel = jnp.zeros(o_ref.shape, jnp.int32)
    for j in range(MOBA_TOPK):
        mx = jnp.max(gate, axis=1, keepdims=True)
        idx = jnp.min(jnp.where(gate == mx, blk, nbk), axis=1, keepdims=True)
        sel = jnp.where(out_lane == j, idx, sel)
        gate = jnp.where(blk == idx, -jnp.inf, gate)
    o_ref[...] = sel


def _moba_select(kmean_t, q3, n_heads):
    n_dec, _, nbk = kmean_t.shape
    return pl.pallas_call(
        _moba_select_body, grid=(n_dec,),
        in_specs=[
            pl.BlockSpec((None, D_MODEL, nbk), lambda b: (b, 0, 0)),
            pl.BlockSpec((None, 1, D_MODEL), lambda b: (b, 0, 0)),
        ],
        out_specs=pl.BlockSpec((None, n_heads, LANES), lambda b: (b, 0, 0)),
        out_shape=jax.ShapeDtypeStruct((n_dec, n_heads, LANES), jnp.int32),
        compiler_params=_params(1), name="moba_select",
    )(kmean_t, q3)


def _moba_sample_body(pt_ref, sel_ref, q_ref, kn_ref, vn_ref, kpool, vpool, o_ref, kbuf, vbuf, sem,
                      *, layer, n_heads, ppb):
    b = pl.program_id(0)
    n_chunks = MOBA_TOPK * ppb
    slot = b % 2

    def copies(bb, sl):
        out = []
        for h in range(n_heads):
            for c in range(n_chunks):
                pg = pt_ref[bb, ppb * sel_ref[bb, (c // ppb) * n_heads + h] + c % ppb]
                out.append(pltpu.make_async_copy(kpool.at[layer, pg, h], kbuf.at[sl, h, c], sem.at[0, sl]))
                out.append(pltpu.make_async_copy(vpool.at[layer, pg, h], vbuf.at[sl, h, c], sem.at[1, sl]))
        return out

    @pl.when(b == 0)
    def _():
        for cp in copies(b, slot):
            cp.start()

    @pl.when(b + 1 < pl.num_programs(0))
    def _():
        for cp in copies(b + 1, 1 - slot):
            cp.start()

    for cp in copies(b, slot):
        cp.wait()

    q = q_ref[...] * SCALE
    s_self = jnp.sum(q * kn_ref[...], axis=-1, keepdims=True)
    for h in range(n_heads):
        qh = jnp.broadcast_to(q[h:h + 1, :], (8, q.shape[1])).astype(BF16)
        s = jnp.concatenate([jnp.dot(qh, kbuf[slot, h, c].astype(BF16), preferred_element_type=F32)
                             for c in range(n_chunks)], axis=1)
        ss = s_self[h:h + 1, :]
        m = jnp.maximum(jnp.max(s, axis=1, keepdims=True), ss)
        p = jnp.exp(s - m)
        ps = jnp.exp(ss - m)
        den = jnp.sum(p, axis=1, keepdims=True) + ps
        pb = p.astype(BF16)
        page = kbuf.shape[-1]
        o = ps * vn_ref[h:h + 1, :]
        for c in range(n_chunks):
            o = o + lax.dot_general(pb[:, c * page:(c + 1) * page], vbuf[slot, h, c].astype(BF16), _ABT,
                                    preferred_element_type=F32)
        o_ref[h:h + 1, :] = (o / den)[0:1, :]


def _moba_sample(cache_kt, cache_vt, layer, page_table, sel, q, kn, vn):
    n_dec = page_table.shape[0]
    _, _, heads, dim, page = cache_kt.shape
    ppb = MOBA_BLOCK // page
    n_chunks = MOBA_TOPK * ppb
    row = pl.BlockSpec((None, heads, dim), lambda b, pt, idx: (b, 0, 0))
    grid_spec = pltpu.PrefetchScalarGridSpec(
        num_scalar_prefetch=2, grid=(n_dec,),
        in_specs=[row, row, row, pl.BlockSpec(memory_space=pl.ANY), pl.BlockSpec(memory_space=pl.ANY)],
        out_specs=row,
        scratch_shapes=[pltpu.VMEM((2, heads, n_chunks, dim, page), F32),
                        pltpu.VMEM((2, heads, n_chunks, dim, page), F32),
                        pltpu.SemaphoreType.DMA((2, 2))],
    )
    out = pl.pallas_call(
        functools.partial(_moba_sample_body, layer=layer, n_heads=heads, ppb=ppb), grid_spec=grid_spec,
        out_shape=jax.ShapeDtypeStruct((n_dec, heads, dim), F32),
        compiler_params=_params(1), name="moba_sample",
    )(page_table, sel, q, kn, vn, cache_kt, cache_vt)
    return out.reshape(n_dec, heads * dim)


def _diff_sample_body(pt_ref, q_ref, kn_ref, vn_ref, lq1_ref, lk1_ref, lq2_ref, lk2_ref, subln_ref, *refs,
                      n_heads, page, pps, lam_init):
    k_refs, v_refs = refs[:pps], refs[pps:2 * pps]
    o_ref, m_ref, l_ref, acc_ref = refs[2 * pps:]
    pg = pl.program_id(1)
    nrow = 2 * n_heads
    row = lax.broadcasted_iota(jnp.int32, (nrow, LANES), 0)
    lane = lax.broadcasted_iota(jnp.int32, (nrow, LANES), 1)
    rowcol = lax.broadcasted_iota(jnp.int32, (nrow, 1), 0)

    @pl.when(pg == 0)
    def _():
        m_ref[...] = jnp.full(m_ref.shape, NEG, F32)
        l_ref[...] = jnp.zeros(l_ref.shape, F32)
        acc_ref[...] = jnp.zeros(acc_ref.shape, F32)

    lhs = [jnp.where(row == 2 * h + lane // HEAD_DIM, q_ref[h:h + 1, :] * SCALE, 0.0) for h in range(n_heads)]
    lhs_bf = [x.astype(BF16) for x in lhs]
    parts = []
    for i in range(pps):
        s = None
        for h in range(n_heads):
            kh = k_refs[i][pl.ds(h, page, stride=n_heads), :].astype(BF16)
            d = lax.dot_general(lhs_bf[h], kh, _ABT, preferred_element_type=F32)
            s = d if s is None else s + d
        parts.append(s)
    s = jnp.concatenate(parts, axis=1)
    m_old = m_ref[...]
    m_new = jnp.maximum(m_old, jnp.max(s, axis=1, keepdims=True))
    alpha = jnp.exp(m_old - m_new)
    p = jnp.exp(s - m_new)
    l_ref[...] = alpha * l_ref[...] + jnp.sum(p, axis=1, keepdims=True)
    m_ref[...] = m_new
    acc = alpha * acc_ref[...]
    for i in range(pps):
        pi = p[:, i * page:(i + 1) * page]
        for h in range(n_heads):
            ph = jnp.where(rowcol // 2 == h, pi, 0.0).astype(BF16)
            vh = v_refs[i][pl.ds(h, page, stride=n_heads), :].astype(BF16)
            acc = acc + jnp.dot(ph, vh, preferred_element_type=F32)
    acc_ref[...] = acc

    @pl.when(pg == pl.num_programs(1) - 1)
    def _():
        lam = _lambda_value(lq1_ref, lk1_ref, lq2_ref, lk2_ref, lam_init)
        q_all, kn_all, vn_all = lhs[0], None, None
        for h in range(1, n_heads):
            q_all = q_all + lhs[h]
        for h in range(n_heads):
            sel = rowcol // 2 == h
            kn_h = jnp.where(sel, kn_ref[h:h + 1, :], 0.0)
            vn_h = jnp.where(sel, vn_ref[h:h + 1, :], 0.0)
            kn_all = kn_h if kn_all is None else kn_all + kn_h
            vn_all = vn_h if vn_all is None else vn_all + vn_h
        s_self = jnp.sum(q_all * kn_all, axis=1, keepdims=True)
        m1 = m_ref[...]
        m2 = jnp.maximum(m1, s_self)
        a1 = jnp.exp(m1 - m2)
        p_self = jnp.exp(s_self - m2)
        a = (a1 * acc_ref[...] + p_self * vn_all) / (a1 * l_ref[...] + p_self)
        for h in range(n_heads):
            o_ref[h:h + 1, :] = _diff_finish(a[2 * h:2 * h + 1, :], a[2 * h + 1:2 * h + 2, :], lam,
                                             subln_ref[...], lam_init)


def _diff_sample(pool_k, pool_v, page_table, us4, lams, subln, lam_init, pps=8):
    n_dec, n_pages = page_table.shape
    n_heads = us4.shape[2]
    rows = pool_k.shape[1]
    assert n_pages % pps == 0
    vec = pl.BlockSpec((1, HEAD_DIM), lambda b, p, pt: (0, 0))
    pages = [pl.BlockSpec((None, rows, LANES), lambda b, p, pt, i=i: (pt[b, pps * p + i], 0, 0)) for i in range(pps)]
    grid_spec = pltpu.PrefetchScalarGridSpec(
        num_scalar_prefetch=1, grid=(n_dec, n_pages // pps),
        in_specs=[
            pl.BlockSpec((None, None, n_heads, LANES), lambda b, p, pt: (b, 0, 0, 0)),
            pl.BlockSpec((None, None, n_heads, LANES), lambda b, p, pt: (b, 1, 0, 0)),
            pl.BlockSpec((None, None, n_heads, LANES), lambda b, p, pt: (b, 2, 0, 0)),
            vec, vec, vec, vec,
            pl.BlockSpec((1, LANES), lambda b, p, pt: (0, 0)),
        ] + pages + pages,
        out_specs=pl.BlockSpec((None, n_heads, LANES), lambda b, p, pt: (b, 0, 0)),
        scratch_shapes=[pltpu.VMEM((2 * n_heads, 1), F32), pltpu.VMEM((2 * n_heads, 1), F32),
                        pltpu.VMEM((2 * n_heads, LANES), F32)],
    )
    body = functools.partial(_diff_sample_body, n_heads=n_heads, page=rows // n_heads, pps=pps, lam_init=lam_init)
    out = pl.pallas_call(
        body, grid_spec=grid_spec,
        out_shape=jax.ShapeDtypeStruct((n_dec, n_heads, LANES), F32),
        compiler_params=_params(2), name="diff_sample",
    )(page_table, us4, us4, us4, *lams, subln, *([pool_k] * pps), *([pool_v] * pps))
    return out.reshape(n_dec, D_MODEL)


def _dilated_sample_body(qkv_ref, new_ref, c0_ref, c1_ref, c2_ref, o_ref, n0_ref, n1_ref, n2_ref, *, hps):
    b = pl.program_id(0)
    c_refs = (c0_ref, c1_ref, c2_ref)
    n_refs = (n0_ref, n1_ref, n2_ref)
    n_lane = lax.broadcasted_iota(jnp.int32, (1, new_ref.shape[-1]), 1)
    for h in range(hps):
        ms, ls, accs = [], [], []
        for g, (win, dil) in enumerate(C_PATTERNS):
            rows = c_refs[g].shape[-1]
            q = jnp.broadcast_to(qkv_ref[3 * g, h:h + 1, :] * SCALE, (8, HEAD_DIM))
            kn = qkv_ref[3 * g + 1, h:h + 1, :]
            vn = qkv_ref[3 * g + 2, h:h + 1, :]
            kt = c_refs[g][0, h]
            vt = c_refs[g][1, h]
            lane = lax.broadcasted_iota(jnp.int32, (1, rows), 1)
            sc = jnp.dot(q.astype(BF16), kt.astype(BF16), preferred_element_type=F32)
            if dil > 1:
                sc = jnp.where(lane % dil == 0, sc, NEG)
            s_self = jnp.sum(q * kn, axis=1, keepdims=True)
            m = jnp.maximum(jnp.max(sc, axis=1, keepdims=True), s_self)
            p = jnp.exp(sc - m)
            p_self = jnp.exp(s_self - m)
            ms.append(m)
            ls.append(jnp.sum(p, axis=1, keepdims=True) + p_self)
            accs.append(lax.dot_general(p.astype(BF16), vt.astype(BF16), _ABT, preferred_element_type=F32)
                        + p_self * vn)
            for kv, x in ((0, kt), (1, vt)):
                col = jnp.sum(jnp.where(n_lane == b, new_ref[g, kv, h], 0.0), axis=1, keepdims=True)
                n_refs[g][kv, h] = jnp.where(lane == rows - 1, col, pltpu.roll(x, rows - 1, 1))
        mx = jnp.maximum(jnp.maximum(ms[0], ms[1]), ms[2])
        w = [jnp.exp(ms[g] - mx) for g in range(3)]
        den = w[0] * ls[0] + w[1] * ls[1] + w[2] * ls[2]
        o = (w[0] * accs[0] + w[1] * accs[1] + w[2] * accs[2]) / den
        o_ref[h:h + 1, :] = o[0:1, :]


def _dilated_sample(qkv, new_t, caches_t, hps=4):
    n_dec = qkv.shape[0]
    heads = caches_t[0].shape[2]
    hb = heads // hps
    in_specs = [
        pl.BlockSpec((None, 9, None, hps, HEAD_DIM), lambda b, s: (b, 0, s, 0, 0)),
        pl.BlockSpec((3, 2, hps, HEAD_DIM, n_dec), lambda b, s: (0, 0, s, 0, 0)),
    ]
    cache_specs = [pl.BlockSpec((None, 2, hps, HEAD_DIM, c.shape[-1]), lambda b, s: (b, 0, s, 0, 0)) for c in caches_t]
    outs = pl.pallas_call(
        functools.partial(_dilated_sample_body, hps=hps), grid=(n_dec, hb),
        in_specs=in_specs + cache_specs,
        out_specs=[pl.BlockSpec((None, None, hps, HEAD_DIM), lambda b, s: (b, s, 0, 0))] + cache_specs,
        out_shape=[jax.ShapeDtypeStruct((n_dec, hb, hps, HEAD_DIM), F32)]
        + [jax.ShapeDtypeStruct(c.shape, F32) for c in caches_t],
        compiler_params=_params(2), name="dilated_sample",
    )(qkv, new_t, *caches_t)
    return outs[0].reshape(n_dec, heads * HEAD_DIM), outs[1:]


def _rope_tables(pos):
    half = HEAD_DIM // 2
    inv = ROPE_THETA ** (-jnp.arange(half, dtype=F32) / half)
    ang = pos.astype(F32)[:, None] * inv[None, :]
    cos, sin = jnp.cos(ang), jnp.sin(ang)
    cos = jnp.concatenate([cos, cos], axis=1)
    sin = jnp.concatenate([-sin, sin], axis=1)
    return jnp.tile(cos, (1, LANES // HEAD_DIM)), jnp.tile(sin, (1, LANES // HEAD_DIM))


def _head_gain(g):
    return jnp.tile(g.astype(F32), D_MODEL // HEAD_DIM)


def kernel(x_prompt, x_sample, cache_a_k, cache_a_v, cache_b_k, cache_b_v, cache_c_kv0, cache_c_kv1, cache_c_kv2,
           page_table, norm_a, w_in_a, qn_a, kn_a, w_out_a, norm_b, w_in_b, qn_b, kn_b, lam_q1_b, lam_k1_b,
           lam_q2_b, lam_k2_b, subln_b, w_out_b, norm_c, w_in_c, qn_c, kn_c, w_out_c):
    n, seq, _ = x_prompt.shape
    n_dec, dec_seq, _ = x_sample.shape
    assert dec_seq == 1 and seq % 512 == 0
    page = cache_a_k.shape[2]
    n_pages = page_table.shape[1]
    past = n_pages * page
    n_pool = cache_a_k.shape[1]
    depth = norm_a.shape[0] + norm_b.shape[0] + norm_c.shape[0]
    tm = 1024
    a_heads = D_MODEL // HEAD_DIM
    b_heads = D_MODEL // (2 * HEAD_DIM)
    c_caches = (cache_c_kv0, cache_c_kv1, cache_c_kv2)

    cos_p, sin_p = _rope_tables(jnp.arange(seq, dtype=jnp.int32))
    cos_s, sin_s = _rope_tables(jnp.full((n_dec,), past, jnp.int32))
    bd = (jnp.arange(256)[:, None] // HEAD_DIM == jnp.arange(256)[None, :] // HEAD_DIM).astype(BF16)
    zero_gain = jnp.zeros((D_MODEL,), F32)
    cache_a_kt = jnp.transpose(cache_a_k, (0, 1, 3, 4, 2))
    cache_a_vt = jnp.transpose(cache_a_v, (0, 1, 3, 4, 2))

    hp = x_prompt.reshape(n * seq, D_MODEL)
    hs = x_sample.reshape(n_dec, D_MODEL)
    ak_p, av_p, ak_s, av_s = [], [], [], []
    bk_p, bv_p, bk_s, bv_s = [], [], [], []
    ckv_p = [[] for _ in C_PATTERNS]
    ckv_s = [[] for _ in C_PATTERNS]

    def project_both(norm, w_in, gains, modes):
        w_bf = w_in.astype(BF16)
        gains = gains.reshape(gains.shape[0], 1, COL_TILE)
        up = _project(hp, norm, w_bf, gains, modes, cos_p, sin_p, bd, tm)
        us = _project(hs, norm, w_bf, gains, modes, cos_s, sin_s, bd, n_dec)
        return up, us

    for layer in range(depth):
        kind, j = layer % 3, layer // 3
        if kind == 0:
            gains = jnp.stack([_head_gain(qn_a[j]), _head_gain(kn_a[j]), zero_gain, zero_gain])
            modes = jnp.array([1, 1, 0, 0], jnp.int32)
            up, us = project_both(norm_a[j], w_in_a[j], gains, modes)
            op = _moba_prompt(up.reshape(n, seq, 4 * D_MODEL)).reshape(n * seq, D_MODEL)
            us3 = us.reshape(n_dec, 4, a_heads, HEAD_DIM)
            kmean = _moba_kmean(cache_a_kt, j, page_table)
            sel = _moba_select(kmean, us.reshape(n_dec, 1, 4 * D_MODEL), a_heads)
            sel = jnp.swapaxes(sel[:, :, :MOBA_TOPK], 1, 2).reshape(n_dec, MOBA_TOPK * a_heads)
            os_ = _moba_sample(cache_a_kt, cache_a_vt, j, page_table, sel, us3[:, 0], us3[:, 1], us3[:, 2])
            hp = _out_project(op, up, hp, w_out_a[j].astype(BF16), tm)
            hs = _out_project(os_, us, hs, w_out_a[j].astype(BF16), n_dec)
            ak_p.append(up[:, D_MODEL:2 * D_MODEL].reshape(n, seq, a_heads, HEAD_DIM))
            av_p.append(up[:, 2 * D_MODEL:3 * D_MODEL].reshape(n, seq, a_heads, HEAD_DIM))
            ak_s.append(us[:, D_MODEL:2 * D_MODEL].reshape(n_dec, 1, a_heads, HEAD_DIM))
            av_s.append(us[:, 2 * D_MODEL:3 * D_MODEL].reshape(n_dec, 1, a_heads, HEAD_DIM))
        elif kind == 1:
            lam_init = 0.8 - 0.6 * math.exp(-0.3 * layer)
            gains = jnp.stack([_head_gain(qn_b[j]), _head_gain(kn_b[j]), zero_gain, zero_gain])
            modes = jnp.array([1, 1, 0, 0], jnp.int32)
            up, us = project_both(norm_b[j], w_in_b[j], gains, modes)
            lams = tuple(v[j].astype(F32).reshape(1, HEAD_DIM) for v in (lam_q1_b, lam_k1_b, lam_q2_b, lam_k2_b))
            subln = subln_b[j].astype(F32).reshape(1, LANES)
            op = _diff_prompt(up.reshape(n, seq, 4 * D_MODEL), lams, subln, lam_init).reshape(n * seq, D_MODEL)
            pool_k = cache_b_k[j].reshape(n_pool, page * b_heads, LANES)
            pool_v = cache_b_v[j].reshape(n_pool, page * b_heads, LANES)
            os_ = _diff_sample(pool_k, pool_v, page_table, us.reshape(n_dec, 4, b_heads, LANES), lams, subln, lam_init)
            hp = _out_project(op, up, hp, w_out_b[j].astype(BF16), tm)
            hs = _out_project(os_, us, hs, w_out_b[j].astype(BF16), n_dec)
            bk_p.append(up[:, D_MODEL:2 * D_MODEL].reshape(n, seq, b_heads, LANES))
            bv_p.append(up[:, 2 * D_MODEL:3 * D_MODEL].reshape(n, seq, b_heads, LANES))
            bk_s.append(us[:, D_MODEL:2 * D_MODEL].reshape(n_dec, 1, b_heads, LANES))
            bv_s.append(us[:, 2 * D_MODEL:3 * D_MODEL].reshape(n_dec, 1, b_heads, LANES))
        else:
            gain_rows, mode_vals = [], []
            for g in range(len(C_PATTERNS)):
                gain_rows += [_head_gain(qn_c[j, g]), _head_gain(kn_c[j, g]), zero_gain]
                mode_vals += [1, 1, 0]
            gains = jnp.stack(gain_rows + [zero_gain])
            modes = jnp.array(mode_vals + [0], jnp.int32)
            up, us = project_both(norm_c[j], w_in_c[j], gains, modes)
            e = up.shape[1]
            op = _dilated_prompt(up.reshape(n, seq, e)).reshape(n * seq, D_MODEL)
            n_grp = len(C_PATTERNS)
            hps = 4
            qkv = us[:, :3 * n_grp * D_MODEL].reshape(n_dec, 3 * n_grp, a_heads // hps, hps, HEAD_DIM)
            new_kv = us[:, :3 * n_grp * D_MODEL].reshape(n_dec, n_grp, 3, a_heads, HEAD_DIM)[:, :, 1:]
            new_t = jnp.transpose(new_kv, (1, 2, 3, 4, 0))
            caches_t = []
            for g, (win, dil) in enumerate(C_PATTERNS):
                assert c_caches[g].shape[2] == win and win // dil == 128
                caches_t.append(jnp.transpose(c_caches[g][j], (0, 2, 3, 4, 1)))
            os_, adv = _dilated_sample(qkv, new_t, caches_t, hps)
            hp = _out_project(op, up, hp, w_out_c[j].astype(BF16), tm)
            hs = _out_project(os_, us, hs, w_out_c[j].astype(BF16), n_dec)
            up3 = up.reshape(n, seq, e)
            for g, (win, dil) in enumerate(C_PATTERNS):
                c0 = (3 * g + 1) * D_MODEL
                keep_p = min(win, seq)
                ckv_p[g].append(up3[:, seq - keep_p:, c0:c0 + 2 * D_MODEL].reshape(n, keep_p, 2, a_heads, HEAD_DIM))
                ckv_s[g].append(jnp.transpose(adv[g], (0, 4, 1, 2, 3)))

    return (hp.reshape(n, seq, D_MODEL), hs.reshape(n_dec, 1, D_MODEL),
            jnp.stack(ak_p), jnp.stack(av_p), jnp.stack(ak_s), jnp.stack(av_s),
            jnp.stack(bk_p), jnp.stack(bv_p), jnp.stack(bk_s), jnp.stack(bv_s),
            jnp.stack(ckv_p[0]), jnp.stack(ckv_p[1]), jnp.stack(ckv_p[2]),
            jnp.stack(ckv_s[0]), jnp.stack(ckv_s[1]), jnp.stack(ckv_s[2]))
```
